```python
import jax
import jax.numpy as jnp
from jax import lax
import numpy as np

D_MODEL = 1024
BATCH = 32
SEQ = 2048
DEPTH = 1

HEAD_DIM = 64
N_ATTN_HEADS = (D_MODEL // 2) // HEAD_DIM
ATTN_WIDTH = N_ATTN_HEADS * HEAD_DIM
MOBA_BLOCK = 256
MOBA_TOPK = 3
Q_CHUNK = 128
SGU_WIDTH = D_MODEL // 2
N_SGU_GROUPS = 8
SGU_GROUP_DIM = SGU_WIDTH // N_SGU_GROUPS
SGU_CHUNK = 128
D_FF = 4 * D_MODEL
N_MOD = 6
IN_WIDTH = 3 * ATTN_WIDTH + 2 * SGU_WIDTH + 2 * D_MODEL
EPS = 1e-6

kernel_name = "hybrid_moba_gmlp_adaln_block"


def rmsnorm(x, g):
    xf = x.astype(jnp.float32)
    y = xf * lax.rsqrt(jnp.mean(xf * xf, axis=-1, keepdims=True) + EPS)
    return (y * g.astype(jnp.float32)).astype(x.dtype)


def moba_attention(q, k, v):
    B, S, H, Dh = q.shape
    L = MOBA_BLOCK
    nb = -(-S // L)
    s_pad = nb * L
    nqc = S // Q_CHUNK
    k_sel = min(MOBA_TOPK, nb - 1)
    scale = Dh ** -0.5
    qh = q.transpose(0, 2, 1, 3)
    pad = ((0, 0), (0, 0), (0, s_pad - S), (0, 0))
    kb = jnp.pad(k.transpose(0, 2, 1, 3), pad).reshape(B, H, nb, L, Dh)
    vb = jnp.pad(v.transpose(0, 2, 1, 3), pad).reshape(B, H, nb, L, Dh)
    qblk = jnp.arange(S) // L
    if k_sel > 0:
        kmean = jnp.mean(kb.astype(jnp.float32), axis=3)
        gate = jnp.einsum('bhsd,bhnd->bhsn', qh.astype(jnp.float32), kmean)
        fully_past = jnp.arange(nb)[None, :] < qblk[:, None]
        gate = jnp.where(fully_past[None, None], gate, -jnp.inf)
        _, idx = lax.top_k(gate, k_sel)
    else:
        idx = jnp.zeros((B, H, S, 0), jnp.int32)
    valid = idx < qblk[None, None, :, None]

    def to_chunks(a):
        tail = a.shape[3:]
        a = a.reshape((B, H, nqc, Q_CHUNK) + tail).swapaxes(1, 2)
        return a.reshape((B * nqc, H, Q_CHUNK) + tail)

    q_c = to_chunks(qh)
    idx_c = to_chunks(idx)
    valid_c = to_chunks(valid)
    b_ids = jnp.repeat(jnp.arange(B, dtype=jnp.int32), nqc)
    c_ids = jnp.tile(jnp.arange(nqc, dtype=jnp.int32), B)
    heads = jnp.arange(H)[:, None, None]
    offs_q = jnp.arange(Q_CHUNK)
    offs_k = jnp.arange(L)

    def attend_chunk(args):
        qi, ii, vi, bi, ci = args
        kb_b = lax.dynamic_index_in_dim(kb, bi, 0, keepdims=False)
        vb_b = lax.dynamic_index_in_dim(vb, bi, 0, keepdims=False)
        blk = (ci * Q_CHUNK) // L
        k_own = lax.dynamic_index_in_dim(kb_b, blk, 1, keepdims=False)
        v_own = lax.dynamic_index_in_dim(vb_b, blk, 1, keepdims=False)
        qf = qi.astype(jnp.float32) * scale
        s_own = jnp.einsum('hqd,hld->hql', qf, k_own.astype(jnp.float32))
        causal = (blk * L + offs_k)[None, :] <= (ci * Q_CHUNK + offs_q)[:, None]
        s_own = jnp.where(causal[None], s_own, -jnp.inf)
        k_g = kb_b[heads, ii]
        v_g = vb_b[heads, ii]
        s_g = jnp.einsum('hqd,hqkld->hqkl', qf, k_g.astype(jnp.float32))
        s_g = jnp.where(vi[..., None], s_g, -jnp.inf).reshape(H, Q_CHUNK, k_sel * L)
        p = jax.nn.softmax(jnp.concatenate([s_own, s_g], axis=-1), axis=-1)
        p_own = p[..., :L]
        p_g = p[..., L:].reshape(H, Q_CHUNK, k_sel, L)
        o = (jnp.einsum('hql,hld->hqd', p_own, v_own.astype(jnp.float32))
             + jnp.einsum('hqkl,hqkld->hqd', p_g, v_g.astype(jnp.float32)))
        return o.astype(qi.dtype)

    o = lax.map(attend_chunk, (q_c, idx_c, valid_c, b_ids, c_ids))
    o = o.reshape(B, nqc, H, Q_CHUNK, Dh).transpose(0, 1, 3, 2, 4)
    return o.reshape(B, S, H * Dh)


def spatial_gating(u, vs, g_v, w_s, b_s):
    B, S, _ = u.shape
    nc = S // SGU_CHUNK
    vs = rmsnorm(vs, g_v)
    vg = vs.reshape(B, nc, SGU_CHUNK, N_SGU_GROUPS, SGU_GROUP_DIM)
    w_causal = jnp.tril(w_s)
    z = jnp.einsum('gts,bcsgd->bctgd', w_causal, vg) + b_s.T[None, None, :, :, None]
    return u * z.reshape(B, S, SGU_WIDTH)


def setup_inputs(seed: int = 0) -> dict:
    key = jax.random.key(seed)
    ks = jax.random.split(key, 16)
    f32 = jnp.float32
    nrm = lambda k, shape, s: jax.random.normal(k, shape, f32) * s
    return {
        "x": nrm(ks[0], (BATCH, SEQ, D_MODEL), 1.0),
        "c": nrm(ks[1], (BATCH, D_MODEL), 1.0),
        "w_ada": nrm(ks[2], (DEPTH, D_MODEL, N_MOD * D_MODEL), 0.5 * D_MODEL ** -0.5),
        "b_ada": nrm(ks[3], (DEPTH, N_MOD * D_MODEL), 0.01),
        "g_mix": 1.0 + nrm(ks[4], (DEPTH, D_MODEL), 0.01),
        "w_in": nrm(ks[5], (DEPTH, D_MODEL, IN_WIDTH), D_MODEL ** -0.5),
        "w_proj_attn": nrm(ks[6], (DEPTH, ATTN_WIDTH, D_MODEL), ATTN_WIDTH ** -0.5),
        "g_sgu": 1.0 + nrm(ks[7], (DEPTH, SGU_WIDTH), 0.01),
        "w_sgu": nrm(ks[8], (DEPTH, N_SGU_GROUPS, SGU_CHUNK, SGU_CHUNK), SGU_CHUNK ** -0.5),
        "b_sgu": 1.0 + nrm(ks[9], (DEPTH, N_SGU_GROUPS, SGU_CHUNK), 0.01),
        "w_proj_sgu": nrm(ks[10], (DEPTH, SGU_WIDTH, D_MODEL), SGU_WIDTH ** -0.5),
        "w_out": nrm(ks[11], (DEPTH, D_MODEL, D_MODEL), D_MODEL ** -0.5),
        "g_ffn": 1.0 + nrm(ks[12], (DEPTH, D_MODEL), 0.01),
        "w_ff1": nrm(ks[13], (DEPTH, D_MODEL, D_FF), D_MODEL ** -0.5),
        "w_ff2": nrm(ks[14], (DEPTH, D_FF, D_MODEL), D_FF ** -0.5),
        "g_final": 1.0 + nrm(ks[15], (D_MODEL,), 0.01),
    }


def reference(x, c, w_ada, b_ada, g_mix, w_in, w_proj_attn, g_sgu, w_sgu, b_sgu,
              w_proj_sgu, w_out, g_ffn, w_ff1, w_ff2, g_final):
    B, S, _ = x.shape
    A, W, D = ATTN_WIDTH, SGU_WIDTH, D_MODEL
    splits = [A, 2 * A, 3 * A, 3 * A + W, 3 * A + 2 * W, 3 * A + 2 * W + D]
    c_act = jax.nn.silu(c)
    for l in range(DEPTH):
        mod = (c_act @ w_ada[l] + b_ada[l])[:, None, :]
        shift1, scale1, gate1, shift2, scale2, gate2 = jnp.split(mod, N_MOD, axis=-1)
        h = rmsnorm(x, g_mix[l]) * (1 + scale1) + shift1
        proj = h @ w_in[l]
        q, k, v, u, vs, ga, gb = jnp.split(proj, splits, axis=-1)
        hs = (B, S, N_ATTN_HEADS, HEAD_DIM)
        y_attn = moba_attention(q.reshape(hs), k.reshape(hs), v.reshape(hs)) @ w_proj_attn[l]
        y_sgu = spatial_gating(jax.nn.gelu(u), jax.nn.gelu(vs), g_sgu[l], w_sgu[l], b_sgu[l]) @ w_proj_sgu[l]
        merged = jax.nn.sigmoid(ga) * y_attn + jax.nn.sigmoid(gb) * y_sgu
        x = x + gate1 * (merged @ w_out[l])
        h = rmsnorm(x, g_ffn[l]) * (1 + scale2) + shift2
        x = x + gate2 * (jnp.square(jax.nn.relu(h @ w_ff1[l])) @ w_ff2[l])
    return rmsnorm(x, g_final)
```

```python
import functools
import math

import jax
import jax.numpy as jnp
from jax import lax
from jax.experimental import pallas as pl
from jax.experimental.pallas import tpu as pltpu

HEAD_DIM = 64
MOBA_BLOCK = 256
MOBA_TOPK = 3
N_SGU_GROUPS = 8
SGU_CHUNK = 128
N_MOD = 6
EPS = 1e-6
MASK_BIAS = -1e30
LANES = 128
TOKEN_TILE = 512
FF_CHUNK = 1024
MOD_COL_TILE = 1024
VMEM_LIMIT_BYTES = 56 * 1024 * 1024

F32 = jnp.float32
BF16 = jnp.bfloat16


def _const_spec(shape):
    nd = len(shape)
    return pl.BlockSpec(shape, lambda *_: (0,) * nd, pipeline_mode=pl.Buffered(1))


def _rms_scale(x):
    return lax.rsqrt(jnp.mean(x * x, axis=-1, keepdims=True) + EPS)


def _mod_kernel(c_ref, w_ref, b_ref, o_ref):
    c = c_ref[...]
    c_act = c * jax.nn.sigmoid(c)
    o_ref[...] = jnp.dot(c_act, w_ref[...], preferred_element_type=F32,
                         precision=lax.Precision.HIGHEST) + b_ref[...]


def _modulation(c, w_ada, b_ada):
    bsz, d = c.shape
    n = w_ada.shape[1]
    tn = MOD_COL_TILE
    return pl.pallas_call(
        _mod_kernel,
        grid=(n // tn,),
        in_specs=[pl.BlockSpec((bsz, d), lambda j: (0, 0)),
                  pl.BlockSpec((d, tn), lambda j: (0, j)),
                  pl.BlockSpec((1, tn), lambda j: (0, j))],
        out_specs=pl.BlockSpec((bsz, tn), lambda j: (0, j)),
        out_shape=jax.ShapeDtypeStruct((bsz, n), F32),
        name="mod",
    )(c, w_ada, b_ada.reshape(1, n))


def _inproj_kernel(x_ref, mod_ref, gmix_ref, win_ref, gsgu_ref, wsgu_ref, bfull_ref, wps_ref,
                   qT_ref, k_ref, vT_ref, km_ref, sa_ref, gs_ref, su_ref,
                   *, tm, attn_w, sgu_w, d_model, qscale):
    A, W, D = attn_w, sgu_w, d_model
    x = x_ref[0]
    xn = (x * _rms_scale(x)) * gmix_ref[...]
    h = (xn * (1.0 + mod_ref[0, 1:2, :]) + mod_ref[0, 0:1, :]).astype(BF16)

    def proj(c0, c1):
        return jnp.dot(h, win_ref[:, c0:c1], preferred_element_type=F32)

    qT_ref[0] = (proj(0, A) * qscale).T.astype(BF16)
    k = proj(A, 2 * A)
    k_ref[0] = k.astype(BF16)
    for r in range(tm // MOBA_BLOCK):
        km_ref[0, r] = jnp.mean(k[r * MOBA_BLOCK:(r + 1) * MOBA_BLOCK], axis=0, keepdims=True)
    vT_ref[0] = proj(2 * A, 3 * A).T.astype(BF16)

    u = jax.nn.gelu(proj(3 * A, 3 * A + W))
    vs = jax.nn.gelu(proj(3 * A + W, 3 * A + 2 * W))
    vsn = (vs * _rms_scale(vs)) * gsgu_ref[...]
    T = SGU_CHUNK
    row = lax.broadcasted_iota(jnp.int32, (T, T), 0)
    col = lax.broadcasted_iota(jnp.int32, (T, T), 1)
    lane = lax.broadcasted_iota(jnp.int32, (T, LANES), 1)
    gdim = W // N_SGU_GROUPS
    per_vreg = LANES // gdim
    for p in range(W // LANES):
        wcat = jnp.concatenate(
            [jnp.where(row >= col, wsgu_ref[p * per_vreg + a], 0.0) for a in range(per_vreg)],
            axis=1).astype(BF16)
        for c in range(tm // T):
            vp = vsn[c * T:(c + 1) * T, p * LANES:(p + 1) * LANES]
            stacked = jnp.concatenate(
                [jnp.where((lane >= a * gdim) & (lane < (a + 1) * gdim), vp, 0.0)
                 for a in range(per_vreg)], axis=0).astype(BF16)
            z = jnp.dot(wcat, stacked, preferred_element_type=F32)
            z = z + bfull_ref[:, p * LANES:(p + 1) * LANES]
            su_ref[c * T:(c + 1) * T, p * LANES:(p + 1) * LANES] = (
                u[c * T:(c + 1) * T, p * LANES:(p + 1) * LANES] * z).astype(BF16)
    y_sgu = jnp.dot(su_ref[...], wps_ref[...], preferred_element_type=F32)

    c_ga = 3 * A + 2 * W
    sa_ref[0] = jax.nn.sigmoid(proj(c_ga, c_ga + D)).astype(BF16)
    gs_ref[0] = (jax.nn.sigmoid(proj(c_ga + D, c_ga + 2 * D)) * y_sgu).astype(BF16)


def _inproj(x, mod3, g_mix, w_in, g_sgu, w_sgu, b_full, w_proj_sgu, *, attn_w, sgu_w):
    bsz, seq, d = x.shape
    tm = TOKEN_TILE
    A, W = attn_w, sgu_w
    nb_tile = tm // MOBA_BLOCK
    qscale = (HEAD_DIM ** -0.5) * math.log2(math.e)
    kern = functools.partial(_inproj_kernel, tm=tm, attn_w=A, sgu_w=W, d_model=d, qscale=qscale)
    tok = lambda b, t: (b, t, 0)
    chan = lambda b, t: (b, 0, t)
    return pl.pallas_call(
        kern,
        grid=(bsz, seq // tm),
        in_specs=[pl.BlockSpec((1, tm, d), tok),
                  pl.BlockSpec((1, N_MOD, d), lambda b, t: (b, 0, 0)),
                  _const_spec((1, d)),
                  _const_spec(w_in.shape),
                  _const_spec((1, W)),
                  _const_spec(w_sgu.shape),
                  _const_spec(b_full.shape),
                  _const_spec(w_proj_sgu.shape)],
        out_specs=[pl.BlockSpec((1, A, tm), chan),
                   pl.BlockSpec((1, tm, A), tok),
                   pl.BlockSpec((1, A, tm), chan),
                   pl.BlockSpec((1, nb_tile, 1, A), lambda b, t: (b, t, 0, 0)),
                   pl.BlockSpec((1, tm, d), tok),
                   pl.BlockSpec((1, tm, d), tok)],
        out_shape=[jax.ShapeDtypeStruct((bsz, A, seq), BF16),
                   jax.ShapeDtypeStruct((bsz, seq, A), BF16),
                   jax.ShapeDtypeStruct((bsz, A, seq), BF16),
                   jax.ShapeDtypeStruct((bsz, seq // MOBA_BLOCK, 1, A), F32),
                   jax.ShapeDtypeStruct((bsz, seq, d), BF16),
                   jax.ShapeDtypeStruct((bsz, seq, d), BF16)],
        scratch_shapes=[pltpu.VMEM((tm, W), BF16)],
        compiler_params=pltpu.CompilerParams(
            dimension_semantics=("parallel", "parallel"), vmem_limit_bytes=VMEM_LIMIT_BYTES),
        name="inproj",
    )(x, mod3, g_mix.reshape(1, d), w_in, g_sgu.reshape(1, W), w_sgu, b_full, w_proj_sgu)


def _attn_kernel(qT_ref, k_ref, vT_ref, km_ref, o_ref, s_ref, p_ref, *, seq, topk):
    L = MOBA_BLOCK
    nb = seq // L
    heads = LANES // HEAD_DIM
    krow = lax.broadcasted_iota(jnp.int32, (L, L), 0)
    qcol = lax.broadcasted_iota(jnp.int32, (L, L), 1)
    causal_bias = jnp.where(krow <= qcol, 0.0, MASK_BIAS)
    km = km_ref[0].astype(BF16)
    zeros = jnp.zeros((HEAD_DIM, L), BF16)

    for i in range(nb):
        qs = slice(i * L, (i + 1) * L)
        nk = (i + 1) * L
        o_heads = []
        for hh in range(heads):
            qTh = qT_ref[0, hh * HEAD_DIM:(hh + 1) * HEAD_DIM, qs]
            qz = jnp.concatenate([qTh if a == hh else zeros for a in range(heads)], axis=0)

            bias = [None] * i
            if i > topk:
                g = jnp.dot(km, qz, preferred_element_type=F32)
                blk = lax.broadcasted_iota(jnp.int32, (nb, L), 0)
                for n in range(i):
                    gn = g[n:n + 1, :]
                    ahead = jnp.where(blk < n, jnp.where(g >= gn, 1.0, 0.0),
                                      jnp.where(g > gn, 1.0, 0.0))
                    ahead = jnp.where(blk < i, ahead, 0.0)
                    rank = jnp.sum(ahead, axis=0, keepdims=True)
                    bias[n] = jnp.where(rank < topk, 0.0, MASK_BIAS)

            s_ref[0:nk, :] = jnp.dot(k_ref[0, 0:nk, :], qz, preferred_element_type=F32)
            m = None
            for j in range(i + 1):
                sj = s_ref[j * L:(j + 1) * L, :]
                if j == i:
                    sj = sj + causal_bias
                bm = jnp.max(sj, axis=0, keepdims=True)
                if j < i and bias[j] is not None:
                    bm = bm + bias[j]
                m = bm if m is None else jnp.maximum(m, bm)
            l = None
            for j in range(i + 1):
                sj = s_ref[j * L:(j + 1) * L, :]
                if j == i:
                    sj = sj + causal_bias
                off = m if (j == i or bias[j] is None) else m - bias[j]
                pj = jnp.exp2(sj - off)
                lj = jnp.sum(pj, axis=0, keepdims=True)
                l = lj if l is None else l + lj
                p_ref[j * L:(j + 1) * L, :] = pj.astype(BF16)
            oT = jnp.dot(vT_ref[0, hh * HEAD_DIM:(hh + 1) * HEAD_DIM, 0:nk], p_ref[0:nk, :],
                         preferred_element_type=F32)
            o_heads.append(oT * (1.0 / l))
        o_ref[0, qs, :] = jnp.concatenate(o_heads, axis=0).T.astype(BF16)


def _attention(qT, k, vT, kmean):
    bsz, A, seq = qT.shape
    nb = seq // MOBA_BLOCK
    kern = functools.partial(_attn_kernel, seq=seq, topk=min(MOBA_TOPK, nb - 1))
    chan = lambda b, p: (b, p, 0)
    tok = lambda b, p: (b, 0, p)
    return pl.pallas_call(
        kern,
        grid=(bsz, A // LANES),
        in_specs=[pl.BlockSpec((1, LANES, seq), chan),
                  pl.BlockSpec((1, seq, LANES), tok),
                  pl.BlockSpec((1, LANES, seq), chan),
                  pl.BlockSpec((1, nb, LANES), tok)],
        out_specs=pl.BlockSpec((1, seq, LANES), tok),
        out_shape=jax.ShapeDtypeStruct((bsz, seq, A), BF16),
        scratch_shapes=[pltpu.VMEM((seq, MOBA_BLOCK), F32),
                        pltpu.VMEM((seq, MOBA_BLOCK), BF16)],
        compiler_params=pltpu.CompilerParams(
            dimension_semantics=("parallel", "parallel"), vmem_limit_bytes=VMEM_LIMIT_BYTES),
        name="attn",
    )(qT, k, vT, kmean)


def _post_kernel(x_ref, mod_ref, o_ref, sa_ref, gs_ref, wpa_ref, wout_ref, gffn_ref,
                 wff1_ref, wff2_ref, gfin_ref, out_ref, *, d_ff, final_norm):
    y_attn = jnp.dot(o_ref[0], wpa_ref[...], preferred_element_type=F32)
    merged = (sa_ref[0].astype(F32) * y_attn + gs_ref[0].astype(F32)).astype(BF16)
    x1 = x_ref[0] + mod_ref[0, 2:3, :] * jnp.dot(merged, wout_ref[...], preferred_element_type=F32)

    xn = (x1 * _rms_scale(x1)) * gffn_ref[...]
    h = (xn * (1.0 + mod_ref[0, 4:5, :]) + mod_ref[0, 3:4, :]).astype(BF16)
    acc = None
    for c in range(d_ff // FF_CHUNK):
        cs = slice(c * FF_CHUNK, (c + 1) * FF_CHUNK)
        f = jnp.dot(h, wff1_ref[:, cs], preferred_element_type=F32)
        f = jnp.square(jnp.maximum(f, 0.0)).astype(BF16)
        part = jnp.dot(f, wff2_ref[cs, :], preferred_element_type=F32)
        acc = part if acc is None else acc + part
    x2 = x1 + mod_ref[0, 5:6, :] * acc
    if final_norm:
        x2 = (x2 * _rms_scale(x2)) * gfin_ref[...]
    out_ref[0] = x2


def _post(x, mod3, o, sa, gs, w_proj_attn, w_out, g_ffn, w_ff1, w_ff2, g_final, *, final_norm):
    bsz, seq, d = x.shape
    tm = TOKEN_TILE
    A = o.shape[-1]
    d_ff = w_ff1.shape[1]
    kern = functools.partial(_post_kernel, d_ff=d_ff, final_norm=final_norm)
    tok = lambda b, t: (b, t, 0)
    return pl.pallas_call(
        kern,
        grid=(bsz, seq // tm),
        in_specs=[pl.BlockSpec((1, tm, d), tok),
                  pl.BlockSpec((1, N_MOD, d), lambda b, t: (b, 0, 0)),
                  pl.BlockSpec((1, tm, A), tok),
                  pl.BlockSpec((1, tm, d), tok),
                  pl.BlockSpec((1, tm, d), tok),
                  _const_spec(w_proj_attn.shape),
                  _const_spec(w_out.shape),
                  _const_spec((1, d)),
                  _const_spec(w_ff1.shape),
                  _const_spec(w_ff2.shape),
                  _const_spec((1, d))],
        out_specs=pl.BlockSpec((1, tm, d), tok),
        out_shape=jax.ShapeDtypeStruct((bsz, seq, d), F32),
        compiler_params=pltpu.CompilerParams(
            dimension_semantics=("parallel", "parallel"), vmem_limit_bytes=VMEM_LIMIT_BYTES),
        name="post",
    )(x, mod3, o, sa, gs, w_proj_attn, w_out, g_ffn.reshape(1, d), w_ff1, w_ff2,
      g_final.reshape(1, d))


def kernel(x, c, w_ada, b_ada, g_mix, w_in, w_proj_attn, g_sgu, w_sgu, b_sgu, w_proj_sgu, w_out,
           g_ffn, w_ff1, w_ff2, g_final):
    bsz, seq, d = x.shape
    depth = w_ada.shape[0]
    A = w_proj_attn.shape[1]
    W = w_proj_sgu.shape[1]
    assert seq % TOKEN_TILE == 0 and TOKEN_TILE % MOBA_BLOCK == 0 and TOKEN_TILE % SGU_CHUNK == 0
    assert A % LANES == 0 and W % LANES == 0 and LANES % (W // N_SGU_GROUPS) == 0
    assert w_sgu.shape[1:] == (N_SGU_GROUPS, SGU_CHUNK, SGU_CHUNK)
    assert w_in.shape[2] == 3 * A + 2 * W + 2 * d

    for l in range(depth):
        mod3 = _modulation(c, w_ada[l], b_ada[l]).reshape(bsz, N_MOD, d)
        b_full = jnp.repeat(b_sgu[l].T, W // N_SGU_GROUPS, axis=1)
        qT, k, vT, kmean, sa, gs = _inproj(
            x, mod3, g_mix[l], w_in[l].astype(BF16), g_sgu[l], w_sgu[l], b_full,
            w_proj_sgu[l].astype(BF16), attn_w=A, sgu_w=W)
        o = _attention(qT, k, vT, kmean.reshape(bsz, seq // MOBA_BLOCK, A))
        x = _post(x, mod3, o, sa, gs, w_proj_attn[l].astype(BF16), w_out[l].astype(BF16),
                  g_ffn[l], w_ff1[l].astype(BF16), w_ff2[l].astype(BF16), g_final,
                  final_norm=(l == depth - 1))
    return x
```

```python
import functools
import math

import jax
import jax.numpy as jnp
from jax import lax
from jax.experimental import pallas as pl
from jax.experimental.pallas import tpu as pltpu

HEAD_DIM = 64
MOBA_BLOCK = 256
MOBA_TOPK = 3
N_SGU_GROUPS = 8
SGU_CHUNK = 128
N_MOD = 6
EPS = 1e-6
MASK_BIAS = -1e30
LANES = 128
BF16_ROWS = 16
KEY_TILE = 128
PIPE_DEPTH = 8
TOKEN_TILE = 512
FF_CHUNK = 1024
MOD_COL_TILE = 1024
VMEM_LIMIT_BYTES = 56 * 1024 * 1024

F32 = jnp.float32
BF16 = jnp.bfloat16


def _const_spec(shape):
    nd = len(shape)
    return pl.BlockSpec(shape, lambda *_: (0,) * nd, pipeline_mode=pl.Buffered(1))


def _rms_scale(x):
    return lax.rsqrt(jnp.mean(x * x, axis=-1, keepdims=True) + EPS)


def _mod_kernel(c_ref, w_ref, b_ref, o_ref):
    c = c_ref[...]
    c_act = c * jax.nn.sigmoid(c)
    o_ref[...] = jnp.dot(c_act, w_ref[...], preferred_element_type=F32,
                         precision=lax.Precision.HIGHEST) + b_ref[...]


def _modulation(c, w_ada, b_ada):
    bsz, d = c.shape
    n = w_ada.shape[1]
    tn = MOD_COL_TILE
    return pl.pallas_call(
        _mod_kernel,
        grid=(n // tn,),
        in_specs=[pl.BlockSpec((bsz, d), lambda j: (0, 0)),
                  pl.BlockSpec((d, tn), lambda j: (0, j)),
                  pl.BlockSpec((1, tn), lambda j: (0, j))],
        out_specs=pl.BlockSpec((bsz, tn), lambda j: (0, j)),
        out_shape=jax.ShapeDtypeStruct((bsz, n), F32),
        name="mod",
    )(c, w_ada, b_ada.reshape(1, n))


def _inproj_kernel(x_ref, mod_ref, gmix_ref, win_ref, gsgu_ref, wsgu_ref, bfull_ref, wps_ref,
                   qT_ref, k_ref, vT_ref, km_ref, sa_ref, gs_ref, su_ref,
                   *, tm, attn_w, sgu_w, d_model, qscale):
    A, W, D = attn_w, sgu_w, d_model
    x = x_ref[0]
    xn = (x * _rms_scale(x)) * gmix_ref[...]
    h = (xn * (1.0 + mod_ref[0, 1:2, :]) + mod_ref[0, 0:1, :]).astype(BF16)

    def proj(c0, c1):
        return jnp.dot(h, win_ref[:, c0:c1], preferred_element_type=F32)

    qT_ref[0] = (proj(0, A) * qscale).T.astype(BF16)
    k = proj(A, 2 * A)
    k_ref[0] = k.astype(BF16)
    for r in range(tm // MOBA_BLOCK):
        km_ref[0, r] = jnp.mean(k[r * MOBA_BLOCK:(r + 1) * MOBA_BLOCK], axis=0, keepdims=True)
    vT_ref[0] = proj(2 * A, 3 * A).T.astype(BF16)

    u = jax.nn.gelu(proj(3 * A, 3 * A + W))
    vs = jax.nn.gelu(proj(3 * A + W, 3 * A + 2 * W))
    vsn = (vs * _rms_scale(vs)) * gsgu_ref[...]
    T = SGU_CHUNK
    row = lax.broadcasted_iota(jnp.int32, (T, T), 0)
    col = lax.broadcasted_iota(jnp.int32, (T, T), 1)
    lane = lax.broadcasted_iota(jnp.int32, (T, LANES), 1)
    gdim = W // N_SGU_GROUPS
    per_vreg = LANES // gdim
    for p in range(W // LANES):
        wcat = jnp.concatenate(
            [jnp.where(row >= col, wsgu_ref[p * per_vreg + a], 0.0) for a in range(per_vreg)],
            axis=1).astype(BF16)
        for c in range(tm // T):
            vp = vsn[c * T:(c + 1) * T, p * LANES:(p + 1) * LANES]
            stacked = jnp.concatenate(
                [jnp.where((lane >= a * gdim) & (lane < (a + 1) * gdim), vp, 0.0)
                 for a in range(per_vreg)], axis=0).astype(BF16)
            z = jnp.dot(wcat, stacked, preferred_element_type=F32)
            z = z + bfull_ref[:, p * LANES:(p + 1) * LANES]
            su_ref[c * T:(c + 1) * T, p * LANES:(p + 1) * LANES] = (
                u[c * T:(c + 1) * T, p * LANES:(p + 1) * LANES] * z).astype(BF16)
    y_sgu = jnp.dot(su_ref[...], wps_ref[...], preferred_element_type=F32)

    c_ga = 3 * A + 2 * W
    sa_ref[0] = jax.nn.sigmoid(proj(c_ga, c_ga + D)).astype(BF16)
    gs_ref[0] = (jax.nn.sigmoid(proj(c_ga + D, c_ga + 2 * D)) * y_sgu).astype(BF16)


def _inproj(x, mod3, g_mix, w_in, g_sgu, w_sgu, b_full, w_proj_sgu, *, attn_w, sgu_w):
    bsz, seq, d = x.shape
    tm = TOKEN_TILE
    A, W = attn_w, sgu_w
    nb_tile = tm // MOBA_BLOCK
    qscale = (HEAD_DIM ** -0.5) * math.log2(math.e)
    kern = functools.partial(_inproj_kernel, tm=tm, attn_w=A, sgu_w=W, d_model=d, qscale=qscale)
    tok = lambda b, t: (b, t, 0)
    chan = lambda b, t: (b, 0, t)
    return pl.pallas_call(
        kern,
        grid=(bsz, seq // tm),
        in_specs=[pl.BlockSpec((1, tm, d), tok),
                  pl.BlockSpec((1, N_MOD, d), lambda b, t: (b, 0, 0)),
                  _const_spec((1, d)),
                  _const_spec(w_in.shape),
                  _const_spec((1, W)),
                  _const_spec(w_sgu.shape),
                  _const_spec(b_full.shape),
                  _const_spec(w_proj_sgu.shape)],
        out_specs=[pl.BlockSpec((1, A, tm), chan),
                   pl.BlockSpec((1, tm, A), tok),
                   pl.BlockSpec((1, A, tm), chan),
                   pl.BlockSpec((1, nb_tile, 1, A), lambda b, t: (b, t, 0, 0)),
                   pl.BlockSpec((1, tm, d), tok),
                   pl.BlockSpec((1, tm, d), tok)],
        out_shape=[jax.ShapeDtypeStruct((bsz, A, seq), BF16),
                   jax.ShapeDtypeStruct((bsz, seq, A), BF16),
                   jax.ShapeDtypeStruct((bsz, A, seq), BF16),
                   jax.ShapeDtypeStruct((bsz, seq // MOBA_BLOCK, 1, A), F32),
                   jax.ShapeDtypeStruct((bsz, seq, d), BF16),
                   jax.ShapeDtypeStruct((bsz, seq, d), BF16)],
        scratch_shapes=[pltpu.VMEM((tm, W), BF16)],
        compiler_params=pltpu.CompilerParams(
            dimension_semantics=("parallel", "parallel"), vmem_limit_bytes=VMEM_LIMIT_BYTES),
        name="inproj",
    )(x, mod3, g_mix.reshape(1, d), w_in, g_sgu.reshape(1, W), w_sgu, b_full, w_proj_sgu)


def _attn_kernel(qT_ref, k_ref, vT_ref, km_ref, o_ref, *, seq, topk):
    L = MOBA_BLOCK
    nb = seq // L
    heads = LANES // HEAD_DIM
    krow = lax.broadcasted_iota(jnp.int32, (L, L), 0)
    qcol = lax.broadcasted_iota(jnp.int32, (L, L), 1)
    causal_bias = jnp.where(krow <= qcol, 0.0, MASK_BIAS)
    km = km_ref[0].astype(BF16)
    zeros = jnp.zeros((HEAD_DIM, L), BF16)
    ones = jnp.ones((BF16_ROWS, KEY_TILE), BF16)

    def setup(i, hh):
        qTh = qT_ref[0, hh * HEAD_DIM:(hh + 1) * HEAD_DIM, i * L:(i + 1) * L]
        qz = jnp.concatenate([qTh if a == hh else zeros for a in range(heads)], axis=0)
        bias = [None] * i
        if i > topk:
            g = jnp.dot(km, qz, preferred_element_type=F32)
            blk = lax.broadcasted_iota(jnp.int32, (nb, L), 0)
            for n in range(i):
                gn = g[n:n + 1, :]
                ahead = jnp.where(blk < n, jnp.where(g >= gn, 1.0, 0.0),
                                  jnp.where(g > gn, 1.0, 0.0))
                ahead = jnp.where(blk < i, ahead, 0.0)
                rank = jnp.sum(ahead, axis=0, keepdims=True)
                bias[n] = jnp.where(rank < topk, 0.0, MASK_BIAS)
        return qz, bias

    def score_stage(i, hh, t, qz, bias):
        j = (t * KEY_TILE) // L
        s = jnp.dot(k_ref[0, t * KEY_TILE:(t + 1) * KEY_TILE, :], qz, preferred_element_type=F32)
        if j == i:
            s = s + causal_bias[t * KEY_TILE - i * L:(t + 1) * KEY_TILE - i * L, :]
        mt = jnp.max(s, axis=0, keepdims=True)
        p = jnp.exp2((s - mt).astype(BF16))
        return p, (mt + bias[j] if (j < i and bias[j] is not None) else mt)

    def value_stage(hh, t, p):
        v_aug = jnp.concatenate(
            [vT_ref[0, hh * HEAD_DIM:(hh + 1) * HEAD_DIM, t * KEY_TILE:(t + 1) * KEY_TILE], ones],
            axis=0)
        return jnp.dot(v_aug, p, preferred_element_type=F32)

    def merge(maxes, parts):
        m = functools.reduce(jnp.maximum, maxes)
        r = None
        for mt, part in zip(maxes, parts):
            wpart = jnp.exp2(mt - m) * part
            r = wpart if r is None else r + wpart
        return r[0:HEAD_DIM] * (1.0 / r[HEAD_DIM:HEAD_DIM + 1])

    tiles = [(i, hh, t) for i in range(nb) for hh in range(heads)
             for t in range((i + 1) * L // KEY_TILE)]
    ctx, maxes, parts, outs, in_flight = {}, {}, {}, {}, []
    for step in range(len(tiles) + PIPE_DEPTH):
        if step < len(tiles):
            i, hh, t = tiles[step]
            if t == 0:
                ctx[i, hh] = setup(i, hh)
                maxes[i, hh], parts[i, hh] = [], []
            p, mt = score_stage(i, hh, t, *ctx[i, hh])
            maxes[i, hh].append(mt)
            in_flight.append((i, hh, t, p))
        if step >= PIPE_DEPTH:
            i, hh, t, p = in_flight.pop(0)
            parts[i, hh].append(value_stage(hh, t, p))
            if t == (i + 1) * L // KEY_TILE - 1:
                outs[i, hh] = merge(maxes.pop((i, hh)), parts.pop((i, hh)))
                if hh == heads - 1:
                    o_pair = jnp.concatenate([outs.pop((i, a)) for a in range(heads)], axis=0)
                    o_ref[0, i * L:(i + 1) * L, :] = o_pair.T.astype(BF16)


def _attention(qT, k, vT, kmean):
    bsz, A, seq = qT.shape
    nb = seq // MOBA_BLOCK
    kern = functools.partial(_attn_kernel, seq=seq, topk=min(MOBA_TOPK, nb - 1))
    chan = lambda b, p: (b, p, 0)
    tok = lambda b, p: (b, 0, p)
    return pl.pallas_call(
        kern,
        grid=(bsz, A // LANES),
        in_specs=[pl.BlockSpec((1, LANES, seq), chan),
                  pl.BlockSpec((1, seq, LANES), tok),
                  pl.BlockSpec((1, LANES, seq), chan),
                  pl.BlockSpec((1, nb, LANES), tok)],
        out_specs=pl.BlockSpec((1, seq, LANES), tok),
        out_shape=jax.ShapeDtypeStruct((bsz, seq, A), BF16),
        compiler_params=pltpu.CompilerParams(
            dimension_semantics=("parallel", "parallel"), vmem_limit_bytes=VMEM_LIMIT_BYTES),
        name="attn",
    )(qT, k, vT, kmean)


def _post_kernel(x_ref, mod_ref, o_ref, sa_ref, gs_ref, wpa_ref, wout_ref, gffn_ref,
                 wff1_ref, wff2_ref, gfin_ref, out_ref, *, d_ff, final_norm):
    y_attn = jnp.dot(o_ref[0], wpa_ref[...], preferred_element_type=F32)
    merged = (sa_ref[0].astype(F32) * y_attn + gs_ref[0].astype(F32)).astype(BF16)
    x1 = x_ref[0] + mod_ref[0, 2:3, :] * jnp.dot(merged, wout_ref[...], preferred_element_type=F32)

    xn = (x1 * _rms_scale(x1)) * gffn_ref[...]
    h = (xn * (1.0 + mod_ref[0, 4:5, :]) + mod_ref[0, 3:4, :]).astype(BF16)
    acc = None
    for c in range(d_ff // FF_CHUNK):
        cs = slice(c * FF_CHUNK, (c + 1) * FF_CHUNK)
        f = jnp.dot(h, wff1_ref[:, cs], preferred_element_type=F32)
        f = jnp.square(jnp.maximum(f, 0.0)).astype(BF16)
        part = jnp.dot(f, wff2_ref[cs, :], preferred_element_type=F32)
        acc = part if acc is None else acc + part
    x2 = x1 + mod_ref[0, 5:6, :] * acc
    if final_norm:
        x2 = (x2 * _rms_scale(x2)) * gfin_ref[...]
    out_ref[0] = x2


def _post(x, mod3, o, sa, gs, w_proj_attn, w_out, g_ffn, w_ff1, w_ff2, g_final, *, final_norm):
    bsz, seq, d = x.shape
    tm = TOKEN_TILE
    A = o.shape[-1]
    d_ff = w_ff1.shape[1]
    kern = functools.partial(_post_kernel, d_ff=d_ff, final_norm=final_norm)
    tok = lambda b, t: (b, t, 0)
    return pl.pallas_call(
        kern,
        grid=(bsz, seq // tm),
        in_specs=[pl.BlockSpec((1, tm, d), tok),
                  pl.BlockSpec((1, N_MOD, d), lambda b, t: (b, 0, 0)),
                  pl.BlockSpec((1, tm, A), tok),
                  pl.BlockSpec((1, tm, d), tok),
                  pl.BlockSpec((1, tm, d), tok),
                  _const_spec(w_proj_attn.shape),
                  _const_spec(w_out.shape),
                  _const_spec((1, d)),
                  _const_spec(w_ff1.shape),
                  _const_spec(w_ff2.shape),
                  _const_spec((1, d))],
        out_specs=pl.BlockSpec((1, tm, d), tok),
        out_shape=jax.ShapeDtypeStruct((bsz, seq, d), F32),
        compiler_params=pltpu.CompilerParams(
            dimension_semantics=("parallel", "parallel"), vmem_limit_bytes=VMEM_LIMIT_BYTES),
        name="post",
    )(x, mod3, o, sa, gs, w_proj_attn, w_out, g_ffn.reshape(1, d), w_ff1, w_ff2,
      g_final.reshape(1, d))


def kernel(x, c, w_ada, b_ada, g_mix, w_in, w_proj_attn, g_sgu, w_sgu, b_sgu, w_proj_sgu, w_out,
           g_ffn, w_ff1, w_ff2, g_final):
    bsz, seq, d = x.shape
    depth = w_ada.shape[0]
    A = w_proj_attn.shape[1]
    W = w_proj_sgu.shape[1]
    assert seq % TOKEN_TILE == 0 and TOKEN_TILE % MOBA_BLOCK == 0 and TOKEN_TILE % SGU_CHUNK == 0
    assert A % LANES == 0 and W % LANES == 0 and LANES % (W // N_SGU_GROUPS) == 0
    assert w_sgu.shape[1:] == (N_SGU_GROUPS, SGU_CHUNK, SGU_CHUNK)
    assert w_in.shape[2] == 3 * A + 2 * W + 2 * d

    for l in range(depth):
        mod3 = _modulation(c, w_ada[l], b_ada[l]).reshape(bsz, N_MOD, d)
        b_full = jnp.repeat(b_sgu[l].T, W // N_SGU_GROUPS, axis=1)
        qT, k, vT, kmean, sa, gs = _inproj(
            x, mod3, g_mix[l], w_in[l].astype(BF16), g_sgu[l], w_sgu[l], b_full,
            w_proj_sgu[l].astype(BF16), attn_w=A, sgu_w=W)
        o = _attention(qT, k, vT, kmean.reshape(bsz, seq // MOBA_BLOCK, A))
        x = _post(x, mod3, o, sa, gs, w_proj_attn[l].astype(BF16), w_out[l].astype(BF16),
                  g_ffn[l], w_ff1[l].astype(BF16), w_ff2[l].astype(BF16), g_final,
                  final_norm=(l == depth - 1))
    return x
```

```python
import functools
import math

import jax
import jax.numpy as jnp
from jax import lax
from jax.experimental import pallas as pl
from jax.experimental.pallas import tpu as pltpu

HEAD_DIM = 64
MOBA_BLOCK = 256
MOBA_TOPK = 3
N_SGU_GROUPS = 8
SGU_CHUNK = 128
N_MOD = 6
EPS = 1e-6
MASK_BIAS = -1e30
LANES = 128
BF16_ROWS = 16
KEY_TILE = 128
PIPE_DEPTH = 8
TOKEN_TILE = 512
SUB_TILE = 256
SGU_CHUNKS_PER_DOT = 2
FF_CHUNK = 1024
MOD_COL_TILE = 1024
VMEM_LIMIT_BYTES = 56 * 1024 * 1024

F32 = jnp.float32
BF16 = jnp.bfloat16


def _const_spec(shape):
    nd = len(shape)
    return pl.BlockSpec(shape, lambda *_: (0,) * nd, pipeline_mode=pl.Buffered(1))


def _rms_scale(x):
    return lax.rsqrt(jnp.mean(x * x, axis=-1, keepdims=True) + EPS)


def _mod_kernel(c_ref, w_ref, b_ref, o_ref):
    c = c_ref[...]
    c_act = c * jax.nn.sigmoid(c)
    o_ref[...] = jnp.dot(c_act, w_ref[...], preferred_element_type=F32,
                         precision=lax.Precision.HIGHEST) + b_ref[...]


def _modulation(c, w_ada, b_ada):
    bsz, d = c.shape
    n = w_ada.shape[1]
    tn = MOD_COL_TILE
    return pl.pallas_call(
        _mod_kernel,
        grid=(n // tn,),
        in_specs=[pl.BlockSpec((bsz, d), lambda j: (0, 0)),
                  pl.BlockSpec((d, tn), lambda j: (0, j)),
                  pl.BlockSpec((1, tn), lambda j: (0, j))],
        out_specs=pl.BlockSpec((bsz, tn), lambda j: (0, j)),
        out_shape=jax.ShapeDtypeStruct((bsz, n), F32),
        name="mod",
    )(c, w_ada, b_ada.reshape(1, n))


def _inproj_kernel(x_ref, mod_ref, gmix_ref, win_ref, gsgu_ref, wsgu_ref, bfull_ref, wps_ref,
                   qT_ref, k_ref, vT_ref, km_ref, sa_ref, gs_ref, su_ref,
                   *, tm, attn_w, sgu_w, d_model, qscale):
    A, W, D = attn_w, sgu_w, d_model
    T = SGU_CHUNK
    c_ga = 3 * A + 2 * W
    row = lax.broadcasted_iota(jnp.int32, (T, T), 0)
    col = lax.broadcasted_iota(jnp.int32, (T, T), 1)
    lane = lax.broadcasted_iota(jnp.int32, (T, LANES), 1)
    gdim = W // N_SGU_GROUPS
    per_vreg = LANES // gdim
    wcat = [jnp.concatenate(
        [jnp.where(row >= col, wsgu_ref[p * per_vreg + a], 0.0) for a in range(per_vreg)],
        axis=1).astype(BF16) for p in range(W // LANES)]

    u, vsn, gate_sgu = [], [], []
    for sub in range(tm // SUB_TILE):
        rows = slice(sub * SUB_TILE, (sub + 1) * SUB_TILE)
        x = x_ref[0, rows, :]
        xn = (x * _rms_scale(x)) * gmix_ref[...]
        h = (xn * (1.0 + mod_ref[0, 1:2, :]) + mod_ref[0, 0:1, :]).astype(BF16)

        def proj(c0, c1):
            return jnp.dot(h, win_ref[:, c0:c1], preferred_element_type=F32)

        u.append(jax.nn.gelu(proj(3 * A, 3 * A + W)))
        vs = jax.nn.gelu(proj(3 * A + W, 3 * A + 2 * W))
        vsn.append((vs * _rms_scale(vs)) * gsgu_ref[...])
        qT_ref[0, :, rows] = (proj(0, A) * qscale).T.astype(BF16)
        k = proj(A, 2 * A)
        k_ref[0, rows, :] = k.astype(BF16)
        for r in range(SUB_TILE // MOBA_BLOCK):
            km_ref[0, sub * (SUB_TILE // MOBA_BLOCK) + r] = jnp.mean(
                k[r * MOBA_BLOCK:(r + 1) * MOBA_BLOCK], axis=0, keepdims=True)
        vT_ref[0, :, rows] = proj(2 * A, 3 * A).T.astype(BF16)
        sa_ref[0, rows, :] = jax.nn.sigmoid(proj(c_ga, c_ga + D)).astype(BF16)
        gate_sgu.append(jax.nn.sigmoid(proj(c_ga + D, c_ga + 2 * D)))

    def stacked(v, c, p):
        vp = v[c * T:(c + 1) * T, p * LANES:(p + 1) * LANES]
        return jnp.concatenate(
            [jnp.where((lane >= a * gdim) & (lane < (a + 1) * gdim), vp, 0.0)
             for a in range(per_vreg)], axis=0).astype(BF16)

    for sub in range(tm // SUB_TILE):
        base = sub * SUB_TILE
        for p in range(W // LANES):
            lanes = slice(p * LANES, (p + 1) * LANES)
            for c in range(0, SUB_TILE // T, SGU_CHUNKS_PER_DOT):
                rhs = jnp.concatenate(
                    [stacked(vsn[sub], c + e, p) for e in range(SGU_CHUNKS_PER_DOT)], axis=1)
                z2 = jnp.dot(wcat[p], rhs, preferred_element_type=F32)
                for e in range(SGU_CHUNKS_PER_DOT):
                    rows = slice((c + e) * T, (c + e + 1) * T)
                    z = z2[:, e * LANES:(e + 1) * LANES] + bfull_ref[:, lanes]
                    su_ref[base + (c + e) * T:base + (c + e + 1) * T, lanes] = (
                        u[sub][rows, lanes] * z).astype(BF16)
        y_sgu = jnp.dot(su_ref[base:base + SUB_TILE, :], wps_ref[...], preferred_element_type=F32)
        gs_ref[0, base:base + SUB_TILE, :] = (gate_sgu[sub] * y_sgu).astype(BF16)


def _inproj(x, mod3, g_mix, w_in, g_sgu, w_sgu, b_full, w_proj_sgu, *, attn_w, sgu_w):
    bsz, seq, d = x.shape
    tm = TOKEN_TILE
    A, W = attn_w, sgu_w
    nb_tile = tm // MOBA_BLOCK
    qscale = (HEAD_DIM ** -0.5) * math.log2(math.e)
    kern = functools.partial(_inproj_kernel, tm=tm, attn_w=A, sgu_w=W, d_model=d, qscale=qscale)
    tok = lambda b, t: (b, t, 0)
    chan = lambda b, t: (b, 0, t)
    return pl.pallas_call(
        kern,
        grid=(bsz, seq // tm),
        in_specs=[pl.BlockSpec((1, tm, d), tok),
                  pl.BlockSpec((1, N_MOD, d), lambda b, t: (b, 0, 0)),
                  _const_spec((1, d)),
                  _const_spec(w_in.shape),
                  _const_spec((1, W)),
                  _const_spec(w_sgu.shape),
                  _const_spec(b_full.shape),
                  _const_spec(w_proj_sgu.shape)],
        out_specs=[pl.BlockSpec((1, A, tm), chan),
                   pl.BlockSpec((1, tm, A), tok),
                   pl.BlockSpec((1, A, tm), chan),
                   pl.BlockSpec((1, nb_tile, 1, A), lambda b, t: (b, t, 0, 0)),
                   pl.BlockSpec((1, tm, d), tok),
                   pl.BlockSpec((1, tm, d), tok)],
        out_shape=[jax.ShapeDtypeStruct((bsz, A, seq), BF16),
                   jax.ShapeDtypeStruct((bsz, seq, A), BF16),
                   jax.ShapeDtypeStruct((bsz, A, seq), BF16),
                   jax.ShapeDtypeStruct((bsz, seq // MOBA_BLOCK, 1, A), F32),
                   jax.ShapeDtypeStruct((bsz, seq, d), BF16),
                   jax.ShapeDtypeStruct((bsz, seq, d), BF16)],
        scratch_shapes=[pltpu.VMEM((tm, W), BF16)],
        compiler_params=pltpu.CompilerParams(
            dimension_semantics=("parallel", "parallel"), vmem_limit_bytes=VMEM_LIMIT_BYTES),
        name="inproj",
    )(x, mod3, g_mix.reshape(1, d), w_in, g_sgu.reshape(1, W), w_sgu, b_full, w_proj_sgu)


def _attn_kernel(qT_ref, k_ref, vT_ref, km_ref, o_ref, *, seq, topk):
    L = MOBA_BLOCK
    nb = seq // L
    heads = LANES // HEAD_DIM
    krow = lax.broadcasted_iota(jnp.int32, (L, L), 0)
    qcol = lax.broadcasted_iota(jnp.int32, (L, L), 1)
    causal_bias = jnp.where(krow <= qcol, 0.0, MASK_BIAS)
    km = km_ref[0].astype(BF16)
    zeros = jnp.zeros((HEAD_DIM, L), BF16)
    ones = jnp.ones((BF16_ROWS, KEY_TILE), BF16)

    def setup(i, hh):
        qTh = qT_ref[0, hh * HEAD_DIM:(hh + 1) * HEAD_DIM, i * L:(i + 1) * L]
        qz = jnp.concatenate([qTh if a == hh else zeros for a in range(heads)], axis=0)
        bias = [None] * i
        if i > topk:
            g = jnp.dot(km, qz, preferred_element_type=F32)
            blk = lax.broadcasted_iota(jnp.int32, (nb, L), 0)
            for n in range(i):
                gn = g[n:n + 1, :]
                ahead = jnp.where(blk < n, jnp.where(g >= gn, 1.0, 0.0),
                                  jnp.where(g > gn, 1.0, 0.0))
                ahead = jnp.where(blk < i, ahead, 0.0)
                rank = jnp.sum(ahead, axis=0, keepdims=True)
                bias[n] = jnp.where(rank < topk, 0.0, MASK_BIAS)
        return qz, bias

    def score_stage(i, hh, t, qz, bias):
        j = (t * KEY_TILE) // L
        s = jnp.dot(k_ref[0, t * KEY_TILE:(t + 1) * KEY_TILE, :], qz, preferred_element_type=F32)
        if j == i:
            s = s + causal_bias[t * KEY_TILE - i * L:(t + 1) * KEY_TILE - i * L, :]
        mt = jnp.max(s, axis=0, keepdims=True)
        p = jnp.exp2((s - mt).astype(BF16))
        return p, (mt + bias[j] if (j < i and bias[j] is not None) else mt)

    def value_stage(hh, t, p):
        v_aug = jnp.concatenate(
            [vT_ref[0, hh * HEAD_DIM:(hh + 1) * HEAD_DIM, t * KEY_TILE:(t + 1) * KEY_TILE], ones],
            axis=0)
        return jnp.dot(v_aug, p, preferred_element_type=F32)

    def merge(maxes, parts):
        m = functools.reduce(jnp.maximum, maxes)
        r = None
        for mt, part in zip(maxes, parts):
            wpart = jnp.exp2(mt - m) * part
            r = wpart if r is None else r + wpart
        return r[0:HEAD_DIM] * (1.0 / r[HEAD_DIM:HEAD_DIM + 1])

    tiles = [(i, hh, t) for i in range(nb) for hh in range(heads)
             for t in range((i + 1) * L // KEY_TILE)]
    ctx, maxes, parts, outs, in_flight = {}, {}, {}, {}, []
    for step in range(len(tiles) + PIPE_DEPTH):
        if step < len(tiles):
            i, hh, t = tiles[step]
            if t == 0:
                ctx[i, hh] = setup(i, hh)
                maxes[i, hh], parts[i, hh] = [], []
            p, mt = score_stage(i, hh, t, *ctx[i, hh])
            maxes[i, hh].append(mt)
            in_flight.append((i, hh, t, p))
        if step >= PIPE_DEPTH:
            i, hh, t, p = in_flight.pop(0)
            parts[i, hh].append(value_stage(hh, t, p))
            if t == (i + 1) * L // KEY_TILE - 1:
                outs[i, hh] = merge(maxes.pop((i, hh)), parts.pop((i, hh)))
                if hh == heads - 1:
                    o_pair = jnp.concatenate([outs.pop((i, a)) for a in range(heads)], axis=0)
                    o_ref[0, i * L:(i + 1) * L, :] = o_pair.T.astype(BF16)


def _attention(qT, k, vT, kmean):
    bsz, A, seq = qT.shape
    nb = seq // MOBA_BLOCK
    kern = functools.partial(_attn_kernel, seq=seq, topk=min(MOBA_TOPK, nb - 1))
    chan = lambda b, p: (b, p, 0)
    tok = lambda b, p: (b, 0, p)
    return pl.pallas_call(
        kern,
        grid=(bsz, A // LANES),
        in_specs=[pl.BlockSpec((1, LANES, seq), chan),
                  pl.BlockSpec((1, seq, LANES), tok),
                  pl.BlockSpec((1, LANES, seq), chan),
                  pl.BlockSpec((1, nb, LANES), tok)],
        out_specs=pl.BlockSpec((1, seq, LANES), tok),
        out_shape=jax.ShapeDtypeStruct((bsz, seq, A), BF16),
        compiler_params=pltpu.CompilerParams(
            dimension_semantics=("parallel", "parallel"), vmem_limit_bytes=VMEM_LIMIT_BYTES),
        name="attn",
    )(qT, k, vT, kmean)


def _post_kernel(x_ref, mod_ref, o_ref, sa_ref, gs_ref, wpa_ref, wout_ref, gffn_ref,
                 wff1_ref, wff2_ref, gfin_ref, out_ref, *, tm, d_ff, final_norm):
    x1s, hs = [], []
    for sub in range(tm // SUB_TILE):
        rows = slice(sub * SUB_TILE, (sub + 1) * SUB_TILE)
        y_attn = jnp.dot(o_ref[0, rows, :], wpa_ref[...], preferred_element_type=F32)
        merged = (sa_ref[0, rows, :].astype(F32) * y_attn
                  + gs_ref[0, rows, :].astype(F32)).astype(BF16)
        x1 = x_ref[0, rows, :] + mod_ref[0, 2:3, :] * jnp.dot(
            merged, wout_ref[...], preferred_element_type=F32)
        xn = (x1 * _rms_scale(x1)) * gffn_ref[...]
        hs.append((xn * (1.0 + mod_ref[0, 4:5, :]) + mod_ref[0, 3:4, :]).astype(BF16))
        x1s.append(x1)
    for sub in range(tm // SUB_TILE):
        acc = None
        for c in range(d_ff // FF_CHUNK):
            cs = slice(c * FF_CHUNK, (c + 1) * FF_CHUNK)
            f = jnp.dot(hs[sub], wff1_ref[:, cs], preferred_element_type=F32)
            f = jnp.square(jnp.maximum(f, 0.0)).astype(BF16)
            part = jnp.dot(f, wff2_ref[cs, :], preferred_element_type=F32)
            acc = part if acc is None else acc + part
        x2 = x1s[sub] + mod_ref[0, 5:6, :] * acc
        if final_norm:
            x2 = (x2 * _rms_scale(x2)) * gfin_ref[...]
        out_ref[0, sub * SUB_TILE:(sub + 1) * SUB_TILE, :] = x2


def _post(x, mod3, o, sa, gs, w_proj_attn, w_out, g_ffn, w_ff1, w_ff2, g_final, *, final_norm):
    bsz, seq, d = x.shape
    tm = TOKEN_TILE
    A = o.shape[-1]
    d_ff = w_ff1.shape[1]
    kern = functools.partial(_post_kernel, tm=tm, d_ff=d_ff, final_norm=final_norm)
    tok = lambda b, t: (b, t, 0)
    return pl.pallas_call(
        kern,
        grid=(bsz, seq // tm),
        in_specs=[pl.BlockSpec((1, tm, d), tok),
                  pl.BlockSpec((1, N_MOD, d), lambda b, t: (b, 0, 0)),
                  pl.BlockSpec((1, tm, A), tok),
                  pl.BlockSpec((1, tm, d), tok),
                  pl.BlockSpec((1, tm, d), tok),
                  _const_spec(w_proj_attn.shape),
                  _const_spec(w_out.shape),
                  _const_spec((1, d)),
                  _const_spec(w_ff1.shape),
                  _const_spec(w_ff2.shape),
                  _const_spec((1, d))],
        out_specs=pl.BlockSpec((1, tm, d), tok),
        out_shape=jax.ShapeDtypeStruct((bsz, seq, d), F32),
        compiler_params=pltpu.CompilerParams(
            dimension_semantics=("parallel", "parallel"), vmem_limit_bytes=VMEM_LIMIT_BYTES),
        name="post",
    )(x, mod3, o, sa, gs, w_proj_attn, w_out, g_ffn.reshape(1, d), w_ff1, w_ff2,
      g_final.reshape(1, d))


def kernel(x, c, w_ada, b_ada, g_mix, w_in, w_proj_attn, g_sgu, w_sgu, b_sgu, w_proj_sgu, w_out,
           g_ffn, w_ff1, w_ff2, g_final):
    bsz, seq, d = x.shape
    depth = w_ada.shape[0]
    A = w_proj_attn.shape[1]
    W = w_proj_sgu.shape[1]
    assert seq % TOKEN_TILE == 0 and TOKEN_TILE % SUB_TILE == 0 and SUB_TILE % MOBA_BLOCK == 0
    assert SUB_TILE % (SGU_CHUNK * SGU_CHUNKS_PER_DOT) == 0
    assert A % LANES == 0 and W % LANES == 0 and LANES % (W // N_SGU_GROUPS) == 0
    assert w_sgu.shape[1:] == (N_SGU_GROUPS, SGU_CHUNK, SGU_CHUNK)
    assert w_in.shape[2] == 3 * A + 2 * W + 2 * d

    for l in range(depth):
        mod3 = _modulation(c, w_ada[l], b_ada[l]).reshape(bsz, N_MOD, d)
        b_full = jnp.repeat(b_sgu[l].T, W // N_SGU_GROUPS, axis=1)
        qT, k, vT, kmean, sa, gs = _inproj(
            x, mod3, g_mix[l], w_in[l].astype(BF16), g_sgu[l], w_sgu[l], b_full,
            w_proj_sgu[l].astype(BF16), attn_w=A, sgu_w=W)
        o = _attention(qT, k, vT, kmean.reshape(bsz, seq // MOBA_BLOCK, A))
        x = _post(x, mod3, o, sa, gs, w_proj_attn[l].astype(BF16), w_out[l].astype(BF16),
                  g_ffn[l], w_ff1[l].astype(BF16), w_ff2[l].astype(BF16), g_final,
                  final_norm=(l == depth - 1))
    return x
```

```python
import functools
import math

import jax
import jax.numpy as jnp
from jax import lax
from jax.experimental import pallas as pl
from jax.experimental.pallas import tpu as pltpu

HEAD_DIM = 64
MOBA_BLOCK = 256
MOBA_TOPK = 3
N_SGU_GROUPS = 8
SGU_CHUNK = 128
N_MOD = 6
EPS = 1e-6
MASK_BIAS = -1e30
LANES = 128
BF16_ROWS = 16
KEY_TILE = 128
PIPE_DEPTH = 12
ATTN_HEADS_PER_STEP = 8
TOKEN_TILE = 512
SUB_TILE = 256
SGU_CHUNKS_PER_DOT = 2
FF_CHUNK = 1024
MOD_COL_TILE = 1024
VMEM_LIMIT_BYTES = 56 * 1024 * 1024

F32 = jnp.float32
BF16 = jnp.bfloat16


def _const_spec(shape):
    nd = len(shape)
    return pl.BlockSpec(shape, lambda *_: (0,) * nd, pipeline_mode=pl.Buffered(1))


def _rms_scale(x):
    return lax.rsqrt(jnp.mean(x * x, axis=-1, keepdims=True) + EPS)


def _mod_kernel(c_ref, w_ref, b_ref, o_ref):
    c = c_ref[...]
    c_act = c * jax.nn.sigmoid(c)
    o_ref[...] = jnp.dot(c_act, w_ref[...], preferred_element_type=F32,
                         precision=lax.Precision.HIGHEST) + b_ref[...]


def _modulation(c, w_ada, b_ada):
    bsz, d = c.shape
    n = w_ada.shape[1]
    tn = MOD_COL_TILE
    return pl.pallas_call(
        _mod_kernel,
        grid=(n // tn,),
        in_specs=[pl.BlockSpec((bsz, d), lambda j: (0, 0)),
                  pl.BlockSpec((d, tn), lambda j: (0, j)),
                  pl.BlockSpec((1, tn), lambda j: (0, j))],
        out_specs=pl.BlockSpec((bsz, tn), lambda j: (0, j)),
        out_shape=jax.ShapeDtypeStruct((bsz, n), F32),
        name="mod",
    )(c, w_ada, b_ada.reshape(1, n))


def _inproj_kernel(x_ref, mod_ref, gmix_ref, win_ref, gsgu_ref, wsgu_ref, bfull_ref, wps_ref,
                   qT_ref, k_ref, vT_ref, km_ref, sa_ref, gs_ref, su_ref,
                   *, tm, attn_w, sgu_w, d_model, qscale):
    A, W, D = attn_w, sgu_w, d_model
    T = SGU_CHUNK
    c_ga = 3 * A + 2 * W
    row = lax.broadcasted_iota(jnp.int32, (T, T), 0)
    col = lax.broadcasted_iota(jnp.int32, (T, T), 1)
    lane = lax.broadcasted_iota(jnp.int32, (T, LANES), 1)
    gdim = W // N_SGU_GROUPS
    per_vreg = LANES // gdim
    wcat = [jnp.concatenate(
        [jnp.where(row >= col, wsgu_ref[p * per_vreg + a], 0.0) for a in range(per_vreg)],
        axis=1).astype(BF16) for p in range(W // LANES)]

    u, vsn, gate_sgu = [], [], []
    for sub in range(tm // SUB_TILE):
        rows = slice(sub * SUB_TILE, (sub + 1) * SUB_TILE)
        x = x_ref[0, rows, :]
        xn = (x * _rms_scale(x)) * gmix_ref[...]
        h = (xn * (1.0 + mod_ref[0, 1:2, :]) + mod_ref[0, 0:1, :]).astype(BF16)

        def proj(c0, c1):
            return jnp.dot(h, win_ref[:, c0:c1], preferred_element_type=F32)

        u.append(jax.nn.gelu(proj(3 * A, 3 * A + W)))
        vs = jax.nn.gelu(proj(3 * A + W, 3 * A + 2 * W))
        vsn.append((vs * _rms_scale(vs)) * gsgu_ref[...])
        qT_ref[0, :, rows] = (proj(0, A) * qscale).T.astype(BF16)
        k = proj(A, 2 * A)
        k_ref[0, rows, :] = k.astype(BF16)
        for r in range(SUB_TILE // MOBA_BLOCK):
            km_ref[0, sub * (SUB_TILE // MOBA_BLOCK) + r] = jnp.mean(
                k[r * MOBA_BLOCK:(r + 1) * MOBA_BLOCK], axis=0, keepdims=True)
        vT_ref[0, :, rows] = proj(2 * A, 3 * A).T.astype(BF16)
        sa_ref[0, rows, :] = jax.nn.sigmoid(proj(c_ga, c_ga + D)).astype(BF16)
        gate_sgu.append(jax.nn.sigmoid(proj(c_ga + D, c_ga + 2 * D)))

    def stacked(v, c, p):
        vp = v[c * T:(c + 1) * T, p * LANES:(p + 1) * LANES]
        return jnp.concatenate(
            [jnp.where((lane >= a * gdim) & (lane < (a + 1) * gdim), vp, 0.0)
             for a in range(per_vreg)], axis=0).astype(BF16)

    for sub in range(tm // SUB_TILE):
        base = sub * SUB_TILE
        for p in range(W // LANES):
            lanes = slice(p * LANES, (p + 1) * LANES)
            for c in range(0, SUB_TILE // T, SGU_CHUNKS_PER_DOT):
                rhs = jnp.concatenate(
                    [stacked(vsn[sub], c + e, p) for e in range(SGU_CHUNKS_PER_DOT)], axis=1)
                z2 = jnp.dot(wcat[p], rhs, preferred_element_type=F32)
                for e in range(SGU_CHUNKS_PER_DOT):
                    rows = slice((c + e) * T, (c + e + 1) * T)
                    z = z2[:, e * LANES:(e + 1) * LANES] + bfull_ref[:, lanes]
                    su_ref[base + (c + e) * T:base + (c + e + 1) * T, lanes] = (
                        u[sub][rows, lanes] * z).astype(BF16)
        y_sgu = jnp.dot(su_ref[base:base + SUB_TILE, :], wps_ref[...], preferred_element_type=F32)
        gs_ref[0, base:base + SUB_TILE, :] = (gate_sgu[sub] * y_sgu).astype(BF16)


def _inproj(x, mod3, g_mix, w_in, g_sgu, w_sgu, b_full, w_proj_sgu, *, attn_w, sgu_w):
    bsz, seq, d = x.shape
    tm = TOKEN_TILE
    A, W = attn_w, sgu_w
    nb_tile = tm // MOBA_BLOCK
    qscale = (HEAD_DIM ** -0.5) * math.log2(math.e)
    kern = functools.partial(_inproj_kernel, tm=tm, attn_w=A, sgu_w=W, d_model=d, qscale=qscale)
    tok = lambda b, t: (b, t, 0)
    chan = lambda b, t: (b, 0, t)
    return pl.pallas_call(
        kern,
        grid=(bsz, seq // tm),
        in_specs=[pl.BlockSpec((1, tm, d), tok),
                  pl.BlockSpec((1, N_MOD, d), lambda b, t: (b, 0, 0)),
                  _const_spec((1, d)),
                  _const_spec(w_in.shape),
                  _const_spec((1, W)),
                  _const_spec(w_sgu.shape),
                  _const_spec(b_full.shape),
                  _const_spec(w_proj_sgu.shape)],
        out_specs=[pl.BlockSpec((1, A, tm), chan),
                   pl.BlockSpec((1, tm, A), tok),
                   pl.BlockSpec((1, A, tm), chan),
                   pl.BlockSpec((1, nb_tile, 1, A), lambda b, t: (b, t, 0, 0)),
                   pl.BlockSpec((1, tm, d), tok),
                   pl.BlockSpec((1, tm, d), tok)],
        out_shape=[jax.ShapeDtypeStruct((bsz, A, seq), BF16),
                   jax.ShapeDtypeStruct((bsz, seq, A), BF16),
                   jax.ShapeDtypeStruct((bsz, A, seq), BF16),
                   jax.ShapeDtypeStruct((bsz, seq // MOBA_BLOCK, 1, A), F32),
                   jax.ShapeDtypeStruct((bsz, seq, d), BF16),
                   jax.ShapeDtypeStruct((bsz, seq, d), BF16)],
        scratch_shapes=[pltpu.VMEM((tm, W), BF16)],
        compiler_params=pltpu.CompilerParams(
            dimension_semantics=("parallel", "parallel"), vmem_limit_bytes=VMEM_LIMIT_BYTES),
        name="inproj",
    )(x, mod3, g_mix.reshape(1, d), w_in, g_sgu.reshape(1, W), w_sgu, b_full, w_proj_sgu)


def _attn_kernel(qT_ref, k_ref, vT_ref, km_ref, o_ref, *, seq, topk, n_heads):
    L = MOBA_BLOCK
    nb = seq // L
    per_group = LANES // HEAD_DIM
    krow = lax.broadcasted_iota(jnp.int32, (L, L), 0)
    qcol = lax.broadcasted_iota(jnp.int32, (L, L), 1)
    causal_bias = jnp.where(krow <= qcol, 0.0, MASK_BIAS)
    zeros = jnp.zeros((HEAD_DIM, L), BF16)
    ones = jnp.ones((BF16_ROWS, KEY_TILE), BF16)

    def setup(i, h):
        qTh = qT_ref[0, h * HEAD_DIM:(h + 1) * HEAD_DIM, i * L:(i + 1) * L]
        qz = jnp.concatenate(
            [qTh if a == h % per_group else zeros for a in range(per_group)], axis=0)
        bias = [None] * i
        if i > topk:
            lanes = slice(h // per_group * LANES, (h // per_group + 1) * LANES)
            km = km_ref[0, :, lanes].astype(BF16)
            g = jnp.dot(km, qz, preferred_element_type=F32)
            blk = lax.broadcasted_iota(jnp.int32, (nb, L), 0)
            for n in range(i):
                gn = g[n:n + 1, :]
                ahead = jnp.where(blk < n, jnp.where(g >= gn, 1.0, 0.0),
                                  jnp.where(g > gn, 1.0, 0.0))
                ahead = jnp.where(blk < i, ahead, 0.0)
                rank = jnp.sum(ahead, axis=0, keepdims=True)
                bias[n] = jnp.where(rank < topk, 0.0, MASK_BIAS)
        return qz, bias

    def score_stage(i, h, t, qz, bias):
        j = (t * KEY_TILE) // L
        lanes = slice(h // per_group * LANES, (h // per_group + 1) * LANES)
        s = jnp.dot(k_ref[0, t * KEY_TILE:(t + 1) * KEY_TILE, lanes], qz,
                    preferred_element_type=F32)
        if j == i:
            s = s + causal_bias[t * KEY_TILE - i * L:(t + 1) * KEY_TILE - i * L, :]
        sb = s.astype(BF16)
        mb = jnp.max(sb, axis=0, keepdims=True)
        p = jnp.exp2(sb - mb)
        mt = mb.astype(F32)
        return p, (mt + bias[j] if (j < i and bias[j] is not None) else mt)

    def value_stage(h, t, p):
        v_aug = jnp.concatenate(
            [vT_ref[0, h * HEAD_DIM:(h + 1) * HEAD_DIM, t * KEY_TILE:(t + 1) * KEY_TILE], ones],
            axis=0)
        return jnp.dot(v_aug, p, preferred_element_type=F32)

    def merge(maxes, parts):
        m = functools.reduce(jnp.maximum, maxes)
        r = None
        for mt, part in zip(maxes, parts):
            wpart = jnp.exp2(mt - m) * part
            r = wpart if r is None else r + wpart
        return r[0:HEAD_DIM] * (1.0 / r[HEAD_DIM:HEAD_DIM + 1])

    tiles = [(i, h, t) for grp in range(n_heads // per_group) for i in range(nb)
             for h in range(grp * per_group, (grp + 1) * per_group)
             for t in range((i + 1) * L // KEY_TILE)]
    ctx, maxes, parts, outs, in_flight = {}, {}, {}, {}, []
    for step in range(len(tiles) + PIPE_DEPTH):
        if step < len(tiles):
            i, h, t = tiles[step]
            if t == 0:
                ctx[i, h] = setup(i, h)
                maxes[i, h], parts[i, h] = [], []
            p, mt = score_stage(i, h, t, *ctx[i, h])
            maxes[i, h].append(mt)
            in_flight.append((i, h, t, p))
        if step >= PIPE_DEPTH:
            i, h, t, p = in_flight.pop(0)
            parts[i, h].append(value_stage(h, t, p))
            if t == (i + 1) * L // KEY_TILE - 1:
                outs[i, h] = merge(maxes.pop((i, h)), parts.pop((i, h)))
                if h % per_group == per_group - 1:
                    o_grp = jnp.concatenate(
                        [outs.pop((i, a)) for a in range(h + 1 - per_group, h + 1)], axis=0)
                    o_ref[0, i * L:(i + 1) * L, (h // per_group) * LANES:
                          (h // per_group + 1) * LANES] = o_grp.T.astype(BF16)


def _attention(qT, k, vT, kmean):
    bsz, A, seq = qT.shape
    nb = seq // MOBA_BLOCK
    aw = ATTN_HEADS_PER_STEP * HEAD_DIM
    kern = functools.partial(_attn_kernel, seq=seq, topk=min(MOBA_TOPK, nb - 1),
                             n_heads=ATTN_HEADS_PER_STEP)
    chan = lambda b, p: (b, p, 0)
    tok = lambda b, p: (b, 0, p)
    return pl.pallas_call(
        kern,
        grid=(bsz, A // aw),
        in_specs=[pl.BlockSpec((1, aw, seq), chan),
                  pl.BlockSpec((1, seq, aw), tok),
                  pl.BlockSpec((1, aw, seq), chan),
                  pl.BlockSpec((1, nb, aw), tok)],
        out_specs=pl.BlockSpec((1, seq, aw), tok),
        out_shape=jax.ShapeDtypeStruct((bsz, seq, A), BF16),
        compiler_params=pltpu.CompilerParams(
            dimension_semantics=("parallel", "parallel"), vmem_limit_bytes=VMEM_LIMIT_BYTES),
        name="attn",
    )(qT, k, vT, kmean)


def _post_kernel(x_ref, mod_ref, o_ref, sa_ref, gs_ref, wpa_ref, wout_ref, gffn_ref,
                 wff1_ref, wff2_ref, gfin_ref, out_ref, *, tm, d_ff, final_norm):
    x1s, hs = [], []
    for sub in range(tm // SUB_TILE):
        rows = slice(sub * SUB_TILE, (sub + 1) * SUB_TILE)
        y_attn = jnp.dot(o_ref[0, rows, :], wpa_ref[...], preferred_element_type=F32)
        merged = (sa_ref[0, rows, :].astype(F32) * y_attn
                  + gs_ref[0, rows, :].astype(F32)).astype(BF16)
        x1 = x_ref[0, rows, :] + mod_ref[0, 2:3, :] * jnp.dot(
            merged, wout_ref[...], preferred_element_type=F32)
        xn = (x1 * _rms_scale(x1)) * gffn_ref[...]
        hs.append((xn * (1.0 + mod_ref[0, 4:5, :]) + mod_ref[0, 3:4, :]).astype(BF16))
        x1s.append(x1)
    for sub in range(tm // SUB_TILE):
        acc = None
        for c in range(d_ff // FF_CHUNK):
            cs = slice(c * FF_CHUNK, (c + 1) * FF_CHUNK)
            f = jnp.dot(hs[sub], wff1_ref[:, cs], preferred_element_type=F32)
            f = jnp.square(jnp.maximum(f, 0.0)).astype(BF16)
            part = jnp.dot(f, wff2_ref[cs, :], preferred_element_type=F32)
            acc = part if acc is None else acc + part
        x2 = x1s[sub] + mod_ref[0, 5:6, :] * acc
        if final_norm:
            x2 = (x2 * _rms_scale(x2)) * gfin_ref[...]
        out_ref[0, sub * SUB_TILE:(sub + 1) * SUB_TILE, :] = x2


def _post(x, mod3, o, sa, gs, w_proj_attn, w_out, g_ffn, w_ff1, w_ff2, g_final, *, final_norm):
    bsz, seq, d = x.shape
    tm = TOKEN_TILE
    A = o.shape[-1]
    d_ff = w_ff1.shape[1]
    kern = functools.partial(_post_kernel, tm=tm, d_ff=d_ff, final_norm=final_norm)
    tok = lambda b, t: (b, t, 0)
    return pl.pallas_call(
        kern,
        grid=(bsz, seq // tm),
        in_specs=[pl.BlockSpec((1, tm, d), tok),
                  pl.BlockSpec((1, N_MOD, d), lambda b, t: (b, 0, 0)),
                  pl.BlockSpec((1, tm, A), tok),
                  pl.BlockSpec((1, tm, d), tok),
                  pl.BlockSpec((1, tm, d), tok),
                  _const_spec(w_proj_attn.shape),
                  _const_spec(w_out.shape),
                  _const_spec((1, d)),
                  _const_spec(w_ff1.shape),
                  _const_spec(w_ff2.shape),
                  _const_spec((1, d))],
        out_specs=pl.BlockSpec((1, tm, d), tok),
        out_shape=jax.ShapeDtypeStruct((bsz, seq, d), F32),
        compiler_params=pltpu.CompilerParams(
            dimension_semantics=("parallel", "parallel"), vmem_limit_bytes=VMEM_LIMIT_BYTES),
        name="post",
    )(x, mod3, o, sa, gs, w_proj_attn, w_out, g_ffn.reshape(1, d), w_ff1, w_ff2,
      g_final.reshape(1, d))


def kernel(x, c, w_ada, b_ada, g_mix, w_in, w_proj_attn, g_sgu, w_sgu, b_sgu, w_proj_sgu, w_out,
           g_ffn, w_ff1, w_ff2, g_final):
    bsz, seq, d = x.shape
    depth = w_ada.shape[0]
    A = w_proj_attn.shape[1]
    W = w_proj_sgu.shape[1]
    assert seq % TOKEN_TILE == 0 and TOKEN_TILE % SUB_TILE == 0 and SUB_TILE % MOBA_BLOCK == 0
    assert SUB_TILE % (SGU_CHUNK * SGU_CHUNKS_PER_DOT) == 0
    assert A % (ATTN_HEADS_PER_STEP * HEAD_DIM) == 0
    assert (ATTN_HEADS_PER_STEP * HEAD_DIM) % LANES == 0
    assert A % LANES == 0 and W % LANES == 0 and LANES % (W // N_SGU_GROUPS) == 0
    assert w_sgu.shape[1:] == (N_SGU_GROUPS, SGU_CHUNK, SGU_CHUNK)
    assert w_in.shape[2] == 3 * A + 2 * W + 2 * d

    for l in range(depth):
        mod3 = _modulation(c, w_ada[l], b_ada[l]).reshape(bsz, N_MOD, d)
        b_full = jnp.repeat(b_sgu[l].T, W // N_SGU_GROUPS, axis=1)
        qT, k, vT, kmean, sa, gs = _inproj(
            x, mod3, g_mix[l], w_in[l].astype(BF16), g_sgu[l], w_sgu[l], b_full,
            w_proj_sgu[l].astype(BF16), attn_w=A, sgu_w=W)
        o = _attention(qT, k, vT, kmean.reshape(bsz, seq // MOBA_BLOCK, A))
        x = _post(x, mod3, o, sa, gs, w_proj_attn[l].astype(BF16), w_out[l].astype(BF16),
                  g_ffn[l], w_ff1[l].astype(BF16), w_ff2[l].astype(BF16), g_final,
                  final_norm=(l == depth - 1))
    return x
```

```python
import functools
import math

import jax
import jax.numpy as jnp
from jax import lax
from jax.experimental import pallas as pl
from jax.experimental.pallas import tpu as pltpu

HEAD_DIM = 64
MOBA_BLOCK = 256
MOBA_TOPK = 3
N_SGU_GROUPS = 8
SGU_CHUNK = 128
N_MOD = 6
EPS = 1e-6
MASK_BIAS = -1e30
LANES = 128
BF16_ROWS = 16
KEY_TILE = 128
PIPE_DEPTH = 12
ATTN_HEADS_PER_STEP = 8
INPROJ_TILE = 1024
POST_TILE = 1024
SUB_TILE = 256
SGU_CHUNKS_PER_DOT = 2
FF_CHUNK = 1024
MOD_COL_TILE = 2048
VMEM_LIMIT_BYTES = 56 * 1024 * 1024

F32 = jnp.float32
BF16 = jnp.bfloat16


def _const_spec(shape):
    nd = len(shape)
    return pl.BlockSpec(shape, lambda *_: (0,) * nd, pipeline_mode=pl.Buffered(1))


def _rms_scale(x):
    return lax.rsqrt(jnp.mean(x * x, axis=-1, keepdims=True) + EPS)


def _mod_kernel(c_ref, w_ref, b_ref, o_ref):
    c = c_ref[...]
    c_act = c * jax.nn.sigmoid(c)
    o_ref[...] = jnp.dot(c_act.astype(BF16), w_ref[...].astype(BF16),
                         preferred_element_type=F32) + b_ref[...]


def _modulation(c, w_ada, b_ada):
    bsz, d = c.shape
    n = w_ada.shape[1]
    tn = MOD_COL_TILE
    return pl.pallas_call(
        _mod_kernel,
        grid=(n // tn,),
        in_specs=[pl.BlockSpec((bsz, d), lambda j: (0, 0)),
                  pl.BlockSpec((d, tn), lambda j: (0, j)),
                  pl.BlockSpec((1, tn), lambda j: (0, j))],
        out_specs=pl.BlockSpec((bsz, tn), lambda j: (0, j)),
        out_shape=jax.ShapeDtypeStruct((bsz, n), F32),
        name="mod",
    )(c, w_ada, b_ada.reshape(1, n))


def _inproj_kernel(x_ref, mod_ref, gmix_ref, win_ref, gsgu_ref, wsgu_ref, bfull_ref, wps_ref,
                   qT_ref, k_ref, vT_ref, km_ref, sa_ref, gs_ref, su_ref,
                   *, tm, attn_w, sgu_w, d_model, qscale):
    A, W, D = attn_w, sgu_w, d_model
    T = SGU_CHUNK
    c_ga = 3 * A + 2 * W
    row = lax.broadcasted_iota(jnp.int32, (T, T), 0)
    col = lax.broadcasted_iota(jnp.int32, (T, T), 1)
    lane = lax.broadcasted_iota(jnp.int32, (T, LANES), 1)
    gdim = W // N_SGU_GROUPS
    per_vreg = LANES // gdim
    wcat = [jnp.concatenate(
        [jnp.where(row >= col, wsgu_ref[p * per_vreg + a], 0.0) for a in range(per_vreg)],
        axis=1).astype(BF16) for p in range(W // LANES)]

    u, vsn, gate_sgu = [], [], []
    for sub in range(tm // SUB_TILE):
        rows = slice(sub * SUB_TILE, (sub + 1) * SUB_TILE)
        x = x_ref[0, rows, :]
        xn = (x * _rms_scale(x)) * gmix_ref[...]
        h = (xn * (1.0 + mod_ref[0, 1:2, :]) + mod_ref[0, 0:1, :]).astype(BF16)

        def proj(c0, c1):
            return jnp.dot(h, win_ref[:, c0:c1], preferred_element_type=F32)

        u.append(jax.nn.gelu(proj(3 * A, 3 * A + W)))
        vs = jax.nn.gelu(proj(3 * A + W, 3 * A + 2 * W))
        vsn.append((vs * _rms_scale(vs)) * gsgu_ref[...])
        qT_ref[0, :, rows] = (proj(0, A) * qscale).T.astype(BF16)
        k = proj(A, 2 * A)
        k_ref[0, rows, :] = k.astype(BF16)
        for r in range(SUB_TILE // MOBA_BLOCK):
            km_ref[0, sub * (SUB_TILE // MOBA_BLOCK) + r] = jnp.mean(
                k[r * MOBA_BLOCK:(r + 1) * MOBA_BLOCK], axis=0, keepdims=True)
        vT_ref[0, :, rows] = proj(2 * A, 3 * A).T.astype(BF16)
        sa_ref[0, rows, :] = jax.nn.sigmoid(proj(c_ga, c_ga + D)).astype(BF16)
        gate_sgu.append(jax.nn.sigmoid(proj(c_ga + D, c_ga + 2 * D)))

    def stacked(v, c, p):
        vp = v[c * T:(c + 1) * T, p * LANES:(p + 1) * LANES]
        return jnp.concatenate(
            [jnp.where((lane >= a * gdim) & (lane < (a + 1) * gdim), vp, 0.0)
             for a in range(per_vreg)], axis=0).astype(BF16)

    for sub in range(tm // SUB_TILE):
        base = sub * SUB_TILE
        for p in range(W // LANES):
            lanes = slice(p * LANES, (p + 1) * LANES)
            for c in range(0, SUB_TILE // T, SGU_CHUNKS_PER_DOT):
                rhs = jnp.concatenate(
                    [stacked(vsn[sub], c + e, p) for e in range(SGU_CHUNKS_PER_DOT)], axis=1)
                z2 = jnp.dot(wcat[p], rhs, preferred_element_type=F32)
                for e in range(SGU_CHUNKS_PER_DOT):
                    rows = slice((c + e) * T, (c + e + 1) * T)
                    z = z2[:, e * LANES:(e + 1) * LANES] + bfull_ref[:, lanes]
                    su_ref[base + (c + e) * T:base + (c + e + 1) * T, lanes] = (
                        u[sub][rows, lanes] * z).astype(BF16)
        y_sgu = jnp.dot(su_ref[base:base + SUB_TILE, :], wps_ref[...], preferred_element_type=F32)
        gs_ref[0, base:base + SUB_TILE, :] = (gate_sgu[sub] * y_sgu).astype(BF16)


def _inproj(x, mod3, g_mix, w_in, g_sgu, w_sgu, b_full, w_proj_sgu, *, attn_w, sgu_w):
    bsz, seq, d = x.shape
    tm = INPROJ_TILE
    A, W = attn_w, sgu_w
    nb_tile = tm // MOBA_BLOCK
    qscale = (HEAD_DIM ** -0.5) * math.log2(math.e)
    kern = functools.partial(_inproj_kernel, tm=tm, attn_w=A, sgu_w=W, d_model=d, qscale=qscale)
    tok = lambda b, t: (b, t, 0)
    chan = lambda b, t: (b, 0, t)
    return pl.pallas_call(
        kern,
        grid=(bsz, seq // tm),
        in_specs=[pl.BlockSpec((1, tm, d), tok),
                  pl.BlockSpec((1, N_MOD, d), lambda b, t: (b, 0, 0)),
                  _const_spec((1, d)),
                  _const_spec(w_in.shape),
                  _const_spec((1, W)),
                  _const_spec(w_sgu.shape),
                  _const_spec(b_full.shape),
                  _const_spec(w_proj_sgu.shape)],
        out_specs=[pl.BlockSpec((1, A, tm), chan),
                   pl.BlockSpec((1, tm, A), tok),
                   pl.BlockSpec((1, A, tm), chan),
                   pl.BlockSpec((1, nb_tile, 1, A), lambda b, t: (b, t, 0, 0)),
                   pl.BlockSpec((1, tm, d), tok),
                   pl.BlockSpec((1, tm, d), tok)],
        out_shape=[jax.ShapeDtypeStruct((bsz, A, seq), BF16),
                   jax.ShapeDtypeStruct((bsz, seq, A), BF16),
                   jax.ShapeDtypeStruct((bsz, A, seq), BF16),
                   jax.ShapeDtypeStruct((bsz, seq // MOBA_BLOCK, 1, A), F32),
                   jax.ShapeDtypeStruct((bsz, seq, d), BF16),
                   jax.ShapeDtypeStruct((bsz, seq, d), BF16)],
        scratch_shapes=[pltpu.VMEM((tm, W), BF16)],
        compiler_params=pltpu.CompilerParams(
            dimension_semantics=("parallel", "parallel"), vmem_limit_bytes=VMEM_LIMIT_BYTES),
        name="inproj",
    )(x, mod3, g_mix.reshape(1, d), w_in, g_sgu.reshape(1, W), w_sgu, b_full, w_proj_sgu)


def _attn_kernel(qT_ref, k_ref, vT_ref, km_ref, o_ref, *, seq, topk, n_heads):
    L = MOBA_BLOCK
    nb = seq // L
    per_group = LANES // HEAD_DIM
    krow = lax.broadcasted_iota(jnp.int32, (L, L), 0)
    qcol = lax.broadcasted_iota(jnp.int32, (L, L), 1)
    causal_bias = jnp.where(krow <= qcol, 0.0, MASK_BIAS)
    zeros = jnp.zeros((HEAD_DIM, L), BF16)
    ones = jnp.ones((BF16_ROWS, KEY_TILE), BF16)

    def setup(i, h):
        qTh = qT_ref[0, h * HEAD_DIM:(h + 1) * HEAD_DIM, i * L:(i + 1) * L]
        qz = jnp.concatenate(
            [qTh if a == h % per_group else zeros for a in range(per_group)], axis=0)
        bias = [None] * i
        if i > topk:
            lanes = slice(h // per_group * LANES, (h // per_group + 1) * LANES)
            km = km_ref[0, :, lanes].astype(BF16)
            g = jnp.dot(km, qz, preferred_element_type=F32)
            blk = lax.broadcasted_iota(jnp.int32, (nb, L), 0)
            for n in range(i):
                gn = g[n:n + 1, :]
                ahead = jnp.where(blk < n, jnp.where(g >= gn, 1.0, 0.0),
                                  jnp.where(g > gn, 1.0, 0.0))
                ahead = jnp.where(blk < i, ahead, 0.0)
                rank = jnp.sum(ahead, axis=0, keepdims=True)
                bias[n] = jnp.where(rank < topk, 0.0, MASK_BIAS)
        return qz, bias

    def score_stage(i, h, t, qz, bias):
        j = (t * KEY_TILE) // L
        lanes = slice(h // per_group * LANES, (h // per_group + 1) * LANES)
        s = jnp.dot(k_ref[0, t * KEY_TILE:(t + 1) * KEY_TILE, lanes], qz,
                    preferred_element_type=F32)
        if j == i:
            s = s + causal_bias[t * KEY_TILE - i * L:(t + 1) * KEY_TILE - i * L, :]
        sb = s.astype(BF16)
        mb = jnp.max(sb, axis=0, keepdims=True)
        p = jnp.exp2(sb - mb)
        mt = mb.astype(F32)
        return p, (mt + bias[j] if (j < i and bias[j] is not None) else mt)

    def value_stage(h, t, p):
        v_aug = jnp.concatenate(
            [vT_ref[0, h * HEAD_DIM:(h + 1) * HEAD_DIM, t * KEY_TILE:(t + 1) * KEY_TILE], ones],
            axis=0)
        return jnp.dot(v_aug, p, preferred_element_type=F32)

    def merge(maxes, parts):
        m = functools.reduce(jnp.maximum, maxes)
        r = None
        for mt, part in zip(maxes, parts):
            wpart = jnp.exp2(mt - m) * part
            r = wpart if r is None else r + wpart
        return r[0:HEAD_DIM] * (1.0 / r[HEAD_DIM:HEAD_DIM + 1])

    tiles = [(i, h, t) for grp in range(n_heads // per_group) for i in range(nb)
             for h in range(grp * per_group, (grp + 1) * per_group)
             for t in range((i + 1) * L // KEY_TILE)]
    ctx, maxes, parts, outs, in_flight = {}, {}, {}, {}, []
    for step in range(len(tiles) + PIPE_DEPTH):
        if step < len(tiles):
            i, h, t = tiles[step]
            if t == 0:
                ctx[i, h] = setup(i, h)
                maxes[i, h], parts[i, h] = [], []
            p, mt = score_stage(i, h, t, *ctx[i, h])
            maxes[i, h].append(mt)
            in_flight.append((i, h, t, p))
        if step >= PIPE_DEPTH:
            i, h, t, p = in_flight.pop(0)
            parts[i, h].append(value_stage(h, t, p))
            if t == (i + 1) * L // KEY_TILE - 1:
                outs[i, h] = merge(maxes.pop((i, h)), parts.pop((i, h)))
                if h % per_group == per_group - 1:
                    o_grp = jnp.concatenate(
                        [outs.pop((i, a)) for a in range(h + 1 - per_group, h + 1)], axis=0)
                    o_ref[0, i * L:(i + 1) * L, (h // per_group) * LANES:
                          (h // per_group + 1) * LANES] = o_grp.T.astype(BF16)


def _attention(qT, k, vT, kmean):
    bsz, A, seq = qT.shape
    nb = seq // MOBA_BLOCK
    aw = ATTN_HEADS_PER_STEP * HEAD_DIM
    kern = functools.partial(_attn_kernel, seq=seq, topk=min(MOBA_TOPK, nb - 1),
                             n_heads=ATTN_HEADS_PER_STEP)
    chan = lambda b, p: (b, p, 0)
    tok = lambda b, p: (b, 0, p)
    return pl.pallas_call(
        kern,
        grid=(bsz, A // aw),
        in_specs=[pl.BlockSpec((1, aw, seq), chan),
                  pl.BlockSpec((1, seq, aw), tok),
                  pl.BlockSpec((1, aw, seq), chan),
                  pl.BlockSpec((1, nb, aw), tok)],
        out_specs=pl.BlockSpec((1, seq, aw), tok),
        out_shape=jax.ShapeDtypeStruct((bsz, seq, A), BF16),
        compiler_params=pltpu.CompilerParams(
            dimension_semantics=("parallel", "parallel"), vmem_limit_bytes=VMEM_LIMIT_BYTES),
        name="attn",
    )(qT, k, vT, kmean)


def _post_kernel(x_ref, mod_ref, o_ref, sa_ref, gs_ref, wpa_ref, wout_ref, gffn_ref,
                 wff1_ref, wff2_ref, gfin_ref, out_ref, *, tm, d_ff, final_norm):
    x1s, hs = [], []
    for sub in range(tm // SUB_TILE):
        rows = slice(sub * SUB_TILE, (sub + 1) * SUB_TILE)
        y_attn = jnp.dot(o_ref[0, rows, :], wpa_ref[...], preferred_element_type=F32)
        merged = (sa_ref[0, rows, :].astype(F32) * y_attn
                  + gs_ref[0, rows, :].astype(F32)).astype(BF16)
        x1 = x_ref[0, rows, :] + mod_ref[0, 2:3, :] * jnp.dot(
            merged, wout_ref[...], preferred_element_type=F32)
        xn = (x1 * _rms_scale(x1)) * gffn_ref[...]
        hs.append((xn * (1.0 + mod_ref[0, 4:5, :]) + mod_ref[0, 3:4, :]).astype(BF16))
        x1s.append(x1)
    for sub in range(tm // SUB_TILE):
        acc = None
        for c in range(d_ff // FF_CHUNK):
            cs = slice(c * FF_CHUNK, (c + 1) * FF_CHUNK)
            f = jnp.dot(hs[sub], wff1_ref[:, cs], preferred_element_type=F32)
            f = jnp.square(jnp.maximum(f, 0.0)).astype(BF16)
            part = jnp.dot(f, wff2_ref[cs, :], preferred_element_type=F32)
            acc = part if acc is None else acc + part
        x2 = x1s[sub] + mod_ref[0, 5:6, :] * acc
        if final_norm:
            x2 = (x2 * _rms_scale(x2)) * gfin_ref[...]
        out_ref[0, sub * SUB_TILE:(sub + 1) * SUB_TILE, :] = x2


def _post(x, mod3, o, sa, gs, w_proj_attn, w_out, g_ffn, w_ff1, w_ff2, g_final, *, final_norm):
    bsz, seq, d = x.shape
    tm = POST_TILE
    A = o.shape[-1]
    d_ff = w_ff1.shape[1]
    kern = functools.partial(_post_kernel, tm=tm, d_ff=d_ff, final_norm=final_norm)
    tok = lambda b, t: (b, t, 0)
    return pl.pallas_call(
        kern,
        grid=(bsz, seq // tm),
        in_specs=[pl.BlockSpec((1, tm, d), tok),
                  pl.BlockSpec((1, N_MOD, d), lambda b, t: (b, 0, 0)),
                  pl.BlockSpec((1, tm, A), tok),
                  pl.BlockSpec((1, tm, d), tok),
                  pl.BlockSpec((1, tm, d), tok),
                  _const_spec(w_proj_attn.shape),
                  _const_spec(w_out.shape),
                  _const_spec((1, d)),
                  _const_spec(w_ff1.shape),
                  _const_spec(w_ff2.shape),
                  _const_spec((1, d))],
        out_specs=pl.BlockSpec((1, tm, d), tok),
        out_shape=jax.ShapeDtypeStruct((bsz, seq, d), F32),
        compiler_params=pltpu.CompilerParams(
            dimension_semantics=("parallel", "parallel"), vmem_limit_bytes=VMEM_LIMIT_BYTES),
        name="post",
    )(x, mod3, o, sa, gs, w_proj_attn, w_out, g_ffn.reshape(1, d), w_ff1, w_ff2,
      g_final.reshape(1, d))


def kernel(x, c, w_ada, b_ada, g_mix, w_in, w_proj_attn, g_sgu, w_sgu, b_sgu, w_proj_sgu, w_out,
           g_ffn, w_ff1, w_ff2, g_final):
    bsz, seq, d = x.shape
    depth = w_ada.shape[0]
    A = w_proj_attn.shape[1]
    W = w_proj_sgu.shape[1]
    assert all(seq % t == 0 and t % SUB_TILE == 0 for t in (INPROJ_TILE, POST_TILE))
    assert SUB_TILE % MOBA_BLOCK == 0
    assert SUB_TILE % (SGU_CHUNK * SGU_CHUNKS_PER_DOT) == 0
    assert A % (ATTN_HEADS_PER_STEP * HEAD_DIM) == 0
    assert (ATTN_HEADS_PER_STEP * HEAD_DIM) % LANES == 0
    assert A % LANES == 0 and W % LANES == 0 and LANES % (W // N_SGU_GROUPS) == 0
    assert w_sgu.shape[1:] == (N_SGU_GROUPS, SGU_CHUNK, SGU_CHUNK)
    assert w_in.shape[2] == 3 * A + 2 * W + 2 * d

    for l in range(depth):
        mod3 = _modulation(c, w_ada[l], b_ada[l]).reshape(bsz, N_MOD, d)
        b_full = jnp.repeat(b_sgu[l].T, W // N_SGU_GROUPS, axis=1)
        qT, k, vT, kmean, sa, gs = _inproj(
            x, mod3, g_mix[l], w_in[l].astype(BF16), g_sgu[l], w_sgu[l], b_full,
            w_proj_sgu[l].astype(BF16), attn_w=A, sgu_w=W)
        o = _attention(qT, k, vT, kmean.reshape(bsz, seq // MOBA_BLOCK, A))
        x = _post(x, mod3, o, sa, gs, w_proj_attn[l].astype(BF16), w_out[l].astype(BF16),
                  g_ffn[l], w_ff1[l].astype(BF16), w_ff2[l].astype(BF16), g_final,
                  final_norm=(l == depth - 1))
    return x
```

```python
import functools
import math

import jax
import jax.numpy as jnp
from jax import lax
from jax.experimental import pallas as pl
from jax.experimental.pallas import tpu as pltpu

HEAD_DIM = 64
MOBA_BLOCK = 256
MOBA_TOPK = 3
N_SGU_GROUPS = 8
SGU_CHUNK = 128
N_MOD = 6
EPS = 1e-6
MASK_BIAS = -1e30
LANES = 128
BF16_ROWS = 16
KEY_TILE = 256
PIPE_DEPTH = 8
ATTN_HEADS_PER_STEP = 8
INPROJ_TILE = 1024
POST_TILE = 1024
SUB_TILE = 256
SGU_CHUNKS_PER_DOT = 2
FF_CHUNK = 1024
MOD_COL_TILE = 2048
VMEM_LIMIT_BYTES = 56 * 1024 * 1024

F32 = jnp.float32
BF16 = jnp.bfloat16


def _const_spec(shape):
    nd = len(shape)
    return pl.BlockSpec(shape, lambda *_: (0,) * nd, pipeline_mode=pl.Buffered(1))


def _rms_scale(x):
    return lax.rsqrt(jnp.mean(x * x, axis=-1, keepdims=True) + EPS)


def _mod_kernel(c_ref, w_ref, b_ref, o_ref):
    c = c_ref[...]
    c_act = c * jax.nn.sigmoid(c)
    o_ref[...] = jnp.dot(c_act.astype(BF16), w_ref[...].astype(BF16),
                         preferred_element_type=F32) + b_ref[...]


def _modulation(c, w_ada, b_ada):
    bsz, d = c.shape
    n = w_ada.shape[1]
    tn = MOD_COL_TILE
    return pl.pallas_call(
        _mod_kernel,
        grid=(n // tn,),
        in_specs=[pl.BlockSpec((bsz, d), lambda j: (0, 0)),
                  pl.BlockSpec((d, tn), lambda j: (0, j)),
                  pl.BlockSpec((1, tn), lambda j: (0, j))],
        out_specs=pl.BlockSpec((bsz, tn), lambda j: (0, j)),
        out_shape=jax.ShapeDtypeStruct((bsz, n), F32),
        name="mod",
    )(c, w_ada, b_ada.reshape(1, n))


def _inproj_kernel(x_ref, mod_ref, gmix_ref, win_ref, gsgu_ref, wsgu_ref, bfull_ref, wps_ref,
                   qT_ref, k_ref, vT_ref, km_ref, sa_ref, gs_ref, su_ref,
                   *, tm, attn_w, sgu_w, d_model, qscale):
    A, W, D = attn_w, sgu_w, d_model
    T = SGU_CHUNK
    c_ga = 3 * A + 2 * W
    row = lax.broadcasted_iota(jnp.int32, (T, T), 0)
    col = lax.broadcasted_iota(jnp.int32, (T, T), 1)
    lane = lax.broadcasted_iota(jnp.int32, (T, LANES), 1)
    gdim = W // N_SGU_GROUPS
    per_vreg = LANES // gdim
    wcat = [jnp.concatenate(
        [jnp.where(row >= col, wsgu_ref[p * per_vreg + a], 0.0) for a in range(per_vreg)],
        axis=1).astype(BF16) for p in range(W // LANES)]

    u, vsn, gate_sgu = [], [], []
    for sub in range(tm // SUB_TILE):
        rows = slice(sub * SUB_TILE, (sub + 1) * SUB_TILE)
        x = x_ref[0, rows, :]
        xn = (x * _rms_scale(x)) * gmix_ref[...]
        h = (xn * (1.0 + mod_ref[0, 1:2, :]) + mod_ref[0, 0:1, :]).astype(BF16)

        def proj(c0, c1):
            return jnp.dot(h, win_ref[:, c0:c1], preferred_element_type=F32)

        u.append(jax.nn.gelu(proj(3 * A, 3 * A + W)))
        vs = jax.nn.gelu(proj(3 * A + W, 3 * A + 2 * W))
        vsn.append((vs * _rms_scale(vs)) * gsgu_ref[...])
        qT_ref[0, :, rows] = (proj(0, A) * qscale).T.astype(BF16)
        k = proj(A, 2 * A)
        k_ref[0, rows, :] = k.astype(BF16)
        for r in range(SUB_TILE // MOBA_BLOCK):
            km_ref[0, sub * (SUB_TILE // MOBA_BLOCK) + r] = jnp.mean(
                k[r * MOBA_BLOCK:(r + 1) * MOBA_BLOCK], axis=0, keepdims=True)
        vT_ref[0, :, rows] = proj(2 * A, 3 * A).T.astype(BF16)
        sa_ref[0, rows, :] = jax.nn.sigmoid(proj(c_ga, c_ga + D)).astype(BF16)
        gate_sgu.append(jax.nn.sigmoid(proj(c_ga + D, c_ga + 2 * D)))

    def stacked(v, c, p):
        vp = v[c * T:(c + 1) * T, p * LANES:(p + 1) * LANES]
        return jnp.concatenate(
            [jnp.where((lane >= a * gdim) & (lane < (a + 1) * gdim), vp, 0.0)
             for a in range(per_vreg)], axis=0).astype(BF16)

    for sub in range(tm // SUB_TILE):
        base = sub * SUB_TILE
        for p in range(W // LANES):
            lanes = slice(p * LANES, (p + 1) * LANES)
            for c in range(0, SUB_TILE // T, SGU_CHUNKS_PER_DOT):
                rhs = jnp.concatenate(
                    [stacked(vsn[sub], c + e, p) for e in range(SGU_CHUNKS_PER_DOT)], axis=1)
                z2 = jnp.dot(wcat[p], rhs, preferred_element_type=F32)
                for e in range(SGU_CHUNKS_PER_DOT):
                    rows = slice((c + e) * T, (c + e + 1) * T)
                    z = z2[:, e * LANES:(e + 1) * LANES] + bfull_ref[:, lanes]
                    su_ref[base + (c + e) * T:base + (c + e + 1) * T, lanes] = (
                        u[sub][rows, lanes] * z).astype(BF16)
        y_sgu = jnp.dot(su_ref[base:base + SUB_TILE, :], wps_ref[...], preferred_element_type=F32)
        gs_ref[0, base:base + SUB_TILE, :] = (gate_sgu[sub] * y_sgu).astype(BF16)


def _inproj(x, mod3, g_mix, w_in, g_sgu, w_sgu, b_full, w_proj_sgu, *, attn_w, sgu_w):
    bsz, seq, d = x.shape
    tm = INPROJ_TILE
    A, W = attn_w, sgu_w
    nb_tile = tm // MOBA_BLOCK
    qscale = (HEAD_DIM ** -0.5) * math.log2(math.e)
    kern = functools.partial(_inproj_kernel, tm=tm, attn_w=A, sgu_w=W, d_model=d, qscale=qscale)
    tok = lambda b, t: (b, t, 0)
    chan = lambda b, t: (b, 0, t)
    return pl.pallas_call(
        kern,
        grid=(bsz, seq // tm),
        in_specs=[pl.BlockSpec((1, tm, d), tok),
                  pl.BlockSpec((1, N_MOD, d), lambda b, t: (b, 0, 0)),
                  _const_spec((1, d)),
                  _const_spec(w_in.shape),
                  _const_spec((1, W)),
                  _const_spec(w_sgu.shape),
                  _const_spec(b_full.shape),
                  _const_spec(w_proj_sgu.shape)],
        out_specs=[pl.BlockSpec((1, A, tm), chan),
                   pl.BlockSpec((1, tm, A), tok),
                   pl.BlockSpec((1, A, tm), chan),
                   pl.BlockSpec((1, nb_tile, 1, A), lambda b, t: (b, t, 0, 0)),
                   pl.BlockSpec((1, tm, d), tok),
                   pl.BlockSpec((1, tm, d), tok)],
        out_shape=[jax.ShapeDtypeStruct((bsz, A, seq), BF16),
                   jax.ShapeDtypeStruct((bsz, seq, A), BF16),
                   jax.ShapeDtypeStruct((bsz, A, seq), BF16),
                   jax.ShapeDtypeStruct((bsz, seq // MOBA_BLOCK, 1, A), F32),
                   jax.ShapeDtypeStruct((bsz, seq, d), BF16),
                   jax.ShapeDtypeStruct((bsz, seq, d), BF16)],
        scratch_shapes=[pltpu.VMEM((tm, W), BF16)],
        compiler_params=pltpu.CompilerParams(
            dimension_semantics=("parallel", "parallel"), vmem_limit_bytes=VMEM_LIMIT_BYTES),
        name="inproj",
    )(x, mod3, g_mix.reshape(1, d), w_in, g_sgu.reshape(1, W), w_sgu, b_full, w_proj_sgu)


def _attn_kernel(qT_ref, k_ref, vT_ref, km_ref, o_ref, *, seq, topk, n_heads):
    L = MOBA_BLOCK
    nb = seq // L
    per_group = LANES // HEAD_DIM
    krow = lax.broadcasted_iota(jnp.int32, (L, L), 0)
    qcol = lax.broadcasted_iota(jnp.int32, (L, L), 1)
    causal_bias = jnp.where(krow <= qcol, 0.0, MASK_BIAS)
    zeros = jnp.zeros((HEAD_DIM, L), BF16)
    ones = jnp.ones((BF16_ROWS, KEY_TILE), BF16)

    def setup(i, h):
        qTh = qT_ref[0, h * HEAD_DIM:(h + 1) * HEAD_DIM, i * L:(i + 1) * L]
        qz = jnp.concatenate(
            [qTh if a == h % per_group else zeros for a in range(per_group)], axis=0)
        bias = [None] * i
        if i > topk:
            lanes = slice(h // per_group * LANES, (h // per_group + 1) * LANES)
            km = km_ref[0, :, lanes].astype(BF16)
            g = jnp.dot(km, qz, preferred_element_type=F32)
            blk = lax.broadcasted_iota(jnp.int32, (nb, L), 0)
            for n in range(i):
                gn = g[n:n + 1, :]
                ahead = jnp.where(blk < n, jnp.where(g >= gn, 1.0, 0.0),
                                  jnp.where(g > gn, 1.0, 0.0))
                ahead = jnp.where(blk < i, ahead, 0.0)
                rank = jnp.sum(ahead, axis=0, keepdims=True)
                bias[n] = jnp.where(rank < topk, 0.0, MASK_BIAS)
        return qz, bias

    def score_stage(i, h, t, qz, bias):
        j = (t * KEY_TILE) // L
        lanes = slice(h // per_group * LANES, (h // per_group + 1) * LANES)
        s = jnp.dot(k_ref[0, t * KEY_TILE:(t + 1) * KEY_TILE, lanes], qz,
                    preferred_element_type=F32)
        if j == i:
            s = s + causal_bias[t * KEY_TILE - i * L:(t + 1) * KEY_TILE - i * L, :]
        sb = s.astype(BF16)
        mb = jnp.max(sb, axis=0, keepdims=True)
        p = jnp.exp2(sb - mb)
        mt = mb.astype(F32)
        return p, (mt + bias[j] if (j < i and bias[j] is not None) else mt)

    def value_stage(h, t, p):
        v_aug = jnp.concatenate(
            [vT_ref[0, h * HEAD_DIM:(h + 1) * HEAD_DIM, t * KEY_TILE:(t + 1) * KEY_TILE], ones],
            axis=0)
        return jnp.dot(v_aug, p, preferred_element_type=F32)

    def merge(maxes, parts):
        m = functools.reduce(jnp.maximum, maxes)
        r = None
        for mt, part in zip(maxes, parts):
            wpart = jnp.exp2(mt - m) * part
            r = wpart if r is None else r + wpart
        return r[0:HEAD_DIM] * (1.0 / r[HEAD_DIM:HEAD_DIM + 1])

    tiles = [(i, h, t) for grp in range(n_heads // per_group) for i in range(nb)
             for h in range(grp * per_group, (grp + 1) * per_group)
             for t in range((i + 1) * L // KEY_TILE)]
    ctx, maxes, parts, outs, in_flight = {}, {}, {}, {}, []
    for step in range(len(tiles) + PIPE_DEPTH):
        if step < len(tiles):
            i, h, t = tiles[step]
            if t == 0:
                ctx[i, h] = setup(i, h)
                maxes[i, h], parts[i, h] = [], []
            p, mt = score_stage(i, h, t, *ctx[i, h])
            maxes[i, h].append(mt)
            in_flight.append((i, h, t, p))
        if step >= PIPE_DEPTH:
            i, h, t, p = in_flight.pop(0)
            parts[i, h].append(value_stage(h, t, p))
            if t == (i + 1) * L // KEY_TILE - 1:
                outs[i, h] = merge(maxes.pop((i, h)), parts.pop((i, h)))
                if h % per_group == per_group - 1:
                    o_grp = jnp.concatenate(
                        [outs.pop((i, a)) for a in range(h + 1 - per_group, h + 1)], axis=0)
                    o_ref[0, i * L:(i + 1) * L, (h // per_group) * LANES:
                          (h // per_group + 1) * LANES] = o_grp.T.astype(BF16)


def _attention(qT, k, vT, kmean):
    bsz, A, seq = qT.shape
    nb = seq // MOBA_BLOCK
    aw = ATTN_HEADS_PER_STEP * HEAD_DIM
    kern = functools.partial(_attn_kernel, seq=seq, topk=min(MOBA_TOPK, nb - 1),
                             n_heads=ATTN_HEADS_PER_STEP)
    chan = lambda b, p: (b, p, 0)
    tok = lambda b, p: (b, 0, p)
    return pl.pallas_call(
        kern,
        grid=(bsz, A // aw),
        in_specs=[pl.BlockSpec((1, aw, seq), chan),
                  pl.BlockSpec((1, seq, aw), tok),
                  pl.BlockSpec((1, aw, seq), chan),
                  pl.BlockSpec((1, nb, aw), tok)],
        out_specs=pl.BlockSpec((1, seq, aw), tok),
        out_shape=jax.ShapeDtypeStruct((bsz, seq, A), BF16),
        compiler_params=pltpu.CompilerParams(
            dimension_semantics=("parallel", "parallel"), vmem_limit_bytes=VMEM_LIMIT_BYTES),
        name="attn",
    )(qT, k, vT, kmean)


def _post_kernel(x_ref, mod_ref, o_ref, sa_ref, gs_ref, wpa_ref, wout_ref, gffn_ref,
                 wff1_ref, wff2_ref, gfin_ref, out_ref, *, tm, d_ff, final_norm):
    x1s, hs = [], []
    for sub in range(tm // SUB_TILE):
        rows = slice(sub * SUB_TILE, (sub + 1) * SUB_TILE)
        y_attn = jnp.dot(o_ref[0, rows, :], wpa_ref[...], preferred_element_type=F32)
        merged = (sa_ref[0, rows, :].astype(F32) * y_attn
                  + gs_ref[0, rows, :].astype(F32)).astype(BF16)
        x1 = x_ref[0, rows, :] + mod_ref[0, 2:3, :] * jnp.dot(
            merged, wout_ref[...], preferred_element_type=F32)
        xn = (x1 * _rms_scale(x1)) * gffn_ref[...]
        hs.append((xn * (1.0 + mod_ref[0, 4:5, :]) + mod_ref[0, 3:4, :]).astype(BF16))
        x1s.append(x1)
    for sub in range(tm // SUB_TILE):
        acc = None
        for c in range(d_ff // FF_CHUNK):
            cs = slice(c * FF_CHUNK, (c + 1) * FF_CHUNK)
            f = jnp.dot(hs[sub], wff1_ref[:, cs], preferred_element_type=F32)
            f = jnp.square(jnp.maximum(f, 0.0)).astype(BF16)
            part = jnp.dot(f, wff2_ref[cs, :], preferred_element_type=F32)
            acc = part if acc is None else acc + part
        x2 = x1s[sub] + mod_ref[0, 5:6, :] * acc
        if final_norm:
            x2 = (x2 * _rms_scale(x2)) * gfin_ref[...]
        out_ref[0, sub * SUB_TILE:(sub + 1) * SUB_TILE, :] = x2


def _post(x, mod3, o, sa, gs, w_proj_attn, w_out, g_ffn, w_ff1, w_ff2, g_final, *, final_norm):
    bsz, seq, d = x.shape
    tm = POST_TILE
    A = o.shape[-1]
    d_ff = w_ff1.shape[1]
    kern = functools.partial(_post_kernel, tm=tm, d_ff=d_ff, final_norm=final_norm)
    tok = lambda b, t: (b, t, 0)
    return pl.pallas_call(
        kern,
        grid=(bsz, seq // tm),
        in_specs=[pl.BlockSpec((1, tm, d), tok),
                  pl.BlockSpec((1, N_MOD, d), lambda b, t: (b, 0, 0)),
                  pl.BlockSpec((1, tm, A), tok),
                  pl.BlockSpec((1, tm, d), tok),
                  pl.BlockSpec((1, tm, d), tok),
                  _const_spec(w_proj_attn.shape),
                  _const_spec(w_out.shape),
                  _const_spec((1, d)),
                  _const_spec(w_ff1.shape),
                  _const_spec(w_ff2.shape),
                  _const_spec((1, d))],
        out_specs=pl.BlockSpec((1, tm, d), tok),
        out_shape=jax.ShapeDtypeStruct((bsz, seq, d), F32),
        compiler_params=pltpu.CompilerParams(
            dimension_semantics=("parallel", "parallel"), vmem_limit_bytes=VMEM_LIMIT_BYTES),
        name="post",
    )(x, mod3, o, sa, gs, w_proj_attn, w_out, g_ffn.reshape(1, d), w_ff1, w_ff2,
      g_final.reshape(1, d))


def kernel(x, c, w_ada, b_ada, g_mix, w_in, w_proj_attn, g_sgu, w_sgu, b_sgu, w_proj_sgu, w_out,
           g_ffn, w_ff1, w_ff2, g_final):
    bsz, seq, d = x.shape
    depth = w_ada.shape[0]
    A = w_proj_attn.shape[1]
    W = w_proj_sgu.shape[1]
    assert all(seq % t == 0 and t % SUB_TILE == 0 for t in (INPROJ_TILE, POST_TILE))
    assert SUB_TILE % MOBA_BLOCK == 0
    assert SUB_TILE % (SGU_CHUNK * SGU_CHUNKS_PER_DOT) == 0
    assert A % (ATTN_HEADS_PER_STEP * HEAD_DIM) == 0
    assert (ATTN_HEADS_PER_STEP * HEAD_DIM) % LANES == 0
    assert A % LANES == 0 and W % LANES == 0 and LANES % (W // N_SGU_GROUPS) == 0
    assert w_sgu.shape[1:] == (N_SGU_GROUPS, SGU_CHUNK, SGU_CHUNK)
    assert w_in.shape[2] == 3 * A + 2 * W + 2 * d

    for l in range(depth):
        mod3 = _modulation(c, w_ada[l], b_ada[l]).reshape(bsz, N_MOD, d)
        b_full = jnp.repeat(b_sgu[l].T, W // N_SGU_GROUPS, axis=1)
        qT, k, vT, kmean, sa, gs = _inproj(
            x, mod3, g_mix[l], w_in[l].astype(BF16), g_sgu[l], w_sgu[l], b_full,
            w_proj_sgu[l].astype(BF16), attn_w=A, sgu_w=W)
        o = _attention(qT, k, vT, kmean.reshape(bsz, seq // MOBA_BLOCK, A))
        x = _post(x, mod3, o, sa, gs, w_proj_attn[l].astype(BF16), w_out[l].astype(BF16),
                  g_ffn[l], w_ff1[l].astype(BF16), w_ff2[l].astype(BF16), g_final,
                  final_norm=(l == depth - 1))
    return x
```

```python
import functools
import math

import jax
import jax.numpy as jnp
from jax import lax
from jax.experimental import pallas as pl
from jax.experimental.pallas import tpu as pltpu

HEAD_DIM = 64
MOBA_BLOCK = 256
MOBA_TOPK = 3
N_SGU_GROUPS = 8
SGU_CHUNK = 128
N_MOD = 6
EPS = 1e-6
MASK_BIAS = -1e30
LANES = 128
BF16_ROWS = 16
KEY_TILE = 256
PIPE_DEPTH = 8
ATTN_HEADS_PER_STEP = 8
INPROJ_TILE = 1024
POST_TILE = 1024
SUB_TILE = 256
SGU_CHUNKS_PER_DOT = 2
FF_CHUNK = 1024
MOD_COL_TILE = 2048
VMEM_LIMIT_BYTES = 56 * 1024 * 1024

F32 = jnp.float32
BF16 = jnp.bfloat16


def _const_spec(shape):
    nd = len(shape)
    return pl.BlockSpec(shape, lambda *_: (0,) * nd, pipeline_mode=pl.Buffered(1))


def _rms_scale(x):
    return lax.rsqrt(jnp.mean(x * x, axis=-1, keepdims=True) + EPS)


def _mod_kernel(c_ref, w_ref, b_ref, o_ref):
    c = c_ref[...]
    c_act = c * jax.nn.sigmoid(c)
    o_ref[...] = jnp.dot(c_act.astype(BF16), w_ref[...].astype(BF16),
                         preferred_element_type=F32) + b_ref[...]


def _modulation(c, w_ada, b_ada):
    bsz, d = c.shape
    n = w_ada.shape[1]
    tn = MOD_COL_TILE
    return pl.pallas_call(
        _mod_kernel,
        grid=(n // tn,),
        in_specs=[pl.BlockSpec((bsz, d), lambda j: (0, 0)),
                  pl.BlockSpec((d, tn), lambda j: (0, j)),
                  pl.BlockSpec((1, tn), lambda j: (0, j))],
        out_specs=pl.BlockSpec((bsz, tn), lambda j: (0, j)),
        out_shape=jax.ShapeDtypeStruct((bsz, n), F32),
        name="mod",
    )(c, w_ada, b_ada.reshape(1, n))


def _inproj_kernel(x_ref, mod_ref, gmix_ref, win_ref, gsgu_ref, wsgu_ref, bfull_ref, wps_ref,
                   qT_ref, k_ref, vT_ref, km_ref, sa_ref, gs_ref, su_ref,
                   *, tm, attn_w, sgu_w, d_model, qscale):
    A, W, D = attn_w, sgu_w, d_model
    T = SGU_CHUNK
    c_ga = 3 * A + 2 * W
    row = lax.broadcasted_iota(jnp.int32, (T, T), 0)
    col = lax.broadcasted_iota(jnp.int32, (T, T), 1)
    lane = lax.broadcasted_iota(jnp.int32, (T, LANES), 1)
    gdim = W // N_SGU_GROUPS
    per_vreg = LANES // gdim
    wcat = [jnp.concatenate(
        [jnp.where(row >= col, wsgu_ref[p * per_vreg + a], 0.0) for a in range(per_vreg)],
        axis=1).astype(BF16) for p in range(W // LANES)]

    u, vsn, gate_sgu = [], [], []
    for sub in range(tm // SUB_TILE):
        rows = slice(sub * SUB_TILE, (sub + 1) * SUB_TILE)
        x = x_ref[0, rows, :]
        xn = (x * _rms_scale(x)) * gmix_ref[...]
        h = (xn * (1.0 + mod_ref[0, 1:2, :]) + mod_ref[0, 0:1, :]).astype(BF16)

        def proj(c0, c1):
            return jnp.dot(h, win_ref[:, c0:c1], preferred_element_type=F32)

        u.append(jax.nn.gelu(proj(3 * A, 3 * A + W)))
        vs = jax.nn.gelu(proj(3 * A + W, 3 * A + 2 * W))
        vsn.append((vs * _rms_scale(vs)) * gsgu_ref[...])
        qT_ref[0, :, rows] = (proj(0, A) * qscale).T.astype(BF16)
        k = proj(A, 2 * A)
        k_ref[0, rows, :] = k.astype(BF16)
        for r in range(SUB_TILE // MOBA_BLOCK):
            km_ref[0, sub * (SUB_TILE // MOBA_BLOCK) + r] = jnp.mean(
                k[r * MOBA_BLOCK:(r + 1) * MOBA_BLOCK], axis=0, keepdims=True)
        vT_ref[0, :, rows] = proj(2 * A, 3 * A).T.astype(BF16)
        sa_ref[0, rows, :] = jax.nn.sigmoid(proj(c_ga, c_ga + D)).astype(BF16)
        gate_sgu.append(jax.nn.sigmoid(proj(c_ga + D, c_ga + 2 * D)))

    def stacked(v, c, p):
        vp = v[c * T:(c + 1) * T, p * LANES:(p + 1) * LANES]
        return jnp.concatenate(
            [jnp.where((lane >= a * gdim) & (lane < (a + 1) * gdim), vp, 0.0)
             for a in range(per_vreg)], axis=0).astype(BF16)

    for sub in range(tm // SUB_TILE):
        base = sub * SUB_TILE
        for p in range(W // LANES):
            lanes = slice(p * LANES, (p + 1) * LANES)
            for c in range(0, SUB_TILE // T, SGU_CHUNKS_PER_DOT):
                rhs = jnp.concatenate(
                    [stacked(vsn[sub], c + e, p) for e in range(SGU_CHUNKS_PER_DOT)], axis=1)
                z2 = jnp.dot(wcat[p], rhs, preferred_element_type=F32)
                for e in range(SGU_CHUNKS_PER_DOT):
                    rows = slice((c + e) * T, (c + e + 1) * T)
                    z = z2[:, e * LANES:(e + 1) * LANES] + bfull_ref[:, lanes]
                    su_ref[base + (c + e) * T:base + (c + e + 1) * T, lanes] = (
                        u[sub][rows, lanes] * z).astype(BF16)
        y_sgu = jnp.dot(su_ref[base:base + SUB_TILE, :], wps_ref[...], preferred_element_type=F32)
        gs_ref[0, base:base + SUB_TILE, :] = (gate_sgu[sub] * y_sgu).astype(BF16)


def _inproj(x, mod3, g_mix, w_in, g_sgu, w_sgu, b_full, w_proj_sgu, *, attn_w, sgu_w):
    bsz, seq, d = x.shape
    tm = INPROJ_TILE
    A, W = attn_w, sgu_w
    nb_tile = tm // MOBA_BLOCK
    qscale = (HEAD_DIM ** -0.5) * math.log2(math.e)
    kern = functools.partial(_inproj_kernel, tm=tm, attn_w=A, sgu_w=W, d_model=d, qscale=qscale)
    tok = lambda b, t: (b, t, 0)
    chan = lambda b, t: (b, 0, t)
    return pl.pallas_call(
        kern,
        grid=(bsz, seq // tm),
        in_specs=[pl.BlockSpec((1, tm, d), tok),
                  pl.BlockSpec((1, N_MOD, d), lambda b, t: (b, 0, 0)),
                  _const_spec((1, d)),
                  _const_spec(w_in.shape),
                  _const_spec((1, W)),
                  _const_spec(w_sgu.shape),
                  _const_spec(b_full.shape),
                  _const_spec(w_proj_sgu.shape)],
        out_specs=[pl.BlockSpec((1, A, tm), chan),
                   pl.BlockSpec((1, tm, A), tok),
                   pl.BlockSpec((1, A, tm), chan),
                   pl.BlockSpec((1, nb_tile, 1, A), lambda b, t: (b, t, 0, 0)),
                   pl.BlockSpec((1, tm, d), tok),
                   pl.BlockSpec((1, tm, d), tok)],
        out_shape=[jax.ShapeDtypeStruct((bsz, A, seq), BF16),
                   jax.ShapeDtypeStruct((bsz, seq, A), BF16),
                   jax.ShapeDtypeStruct((bsz, A, seq), BF16),
                   jax.ShapeDtypeStruct((bsz, seq // MOBA_BLOCK, 1, A), F32),
                   jax.ShapeDtypeStruct((bsz, seq, d), BF16),
                   jax.ShapeDtypeStruct((bsz, seq, d), BF16)],
        scratch_shapes=[pltpu.VMEM((tm, W), BF16)],
        compiler_params=pltpu.CompilerParams(
            dimension_semantics=("parallel", "parallel"), vmem_limit_bytes=VMEM_LIMIT_BYTES),
        name="inproj",
    )(x, mod3, g_mix.reshape(1, d), w_in, g_sgu.reshape(1, W), w_sgu, b_full, w_proj_sgu)


def _attn_kernel(qT_ref, k_ref, vT_ref, km_ref, *rest, seq, topk, n_heads, n_cast):
    o_ref = rest[n_cast]
    for w_ref, wb_ref in zip(rest[:n_cast], rest[n_cast + 1:]):
        wb_ref[...] = w_ref[...].astype(BF16)
    L = MOBA_BLOCK
    nb = seq // L
    per_group = LANES // HEAD_DIM
    krow = lax.broadcasted_iota(jnp.int32, (L, L), 0)
    qcol = lax.broadcasted_iota(jnp.int32, (L, L), 1)
    causal_bias = jnp.where(krow <= qcol, 0.0, MASK_BIAS)
    zeros = jnp.zeros((HEAD_DIM, L), BF16)
    ones = jnp.ones((BF16_ROWS, KEY_TILE), BF16)

    def setup(i, h):
        qTh = qT_ref[0, h * HEAD_DIM:(h + 1) * HEAD_DIM, i * L:(i + 1) * L]
        qz = jnp.concatenate(
            [qTh if a == h % per_group else zeros for a in range(per_group)], axis=0)
        bias = [None] * i
        if i > topk:
            lanes = slice(h // per_group * LANES, (h // per_group + 1) * LANES)
            km = km_ref[0, :, lanes].astype(BF16)
            g = jnp.dot(km, qz, preferred_element_type=F32)
            blk = lax.broadcasted_iota(jnp.int32, (nb, L), 0)
            for n in range(i):
                gn = g[n:n + 1, :]
                ahead = jnp.where(blk < n, jnp.where(g >= gn, 1.0, 0.0),
                                  jnp.where(g > gn, 1.0, 0.0))
                ahead = jnp.where(blk < i, ahead, 0.0)
                rank = jnp.sum(ahead, axis=0, keepdims=True)
                bias[n] = jnp.where(rank < topk, 0.0, MASK_BIAS)
        return qz, bias

    def score_stage(i, h, t, qz, bias):
        j = (t * KEY_TILE) // L
        lanes = slice(h // per_group * LANES, (h // per_group + 1) * LANES)
        s = jnp.dot(k_ref[0, t * KEY_TILE:(t + 1) * KEY_TILE, lanes], qz,
                    preferred_element_type=F32)
        if j == i:
            s = s + causal_bias[t * KEY_TILE - i * L:(t + 1) * KEY_TILE - i * L, :]
        sb = s.astype(BF16)
        mb = jnp.max(sb, axis=0, keepdims=True)
        p = jnp.exp2(sb - mb)
        mt = mb.astype(F32)
        return p, (mt + bias[j] if (j < i and bias[j] is not None) else mt)

    def value_stage(h, t, p):
        v_aug = jnp.concatenate(
            [vT_ref[0, h * HEAD_DIM:(h + 1) * HEAD_DIM, t * KEY_TILE:(t + 1) * KEY_TILE], ones],
            axis=0)
        return jnp.dot(v_aug, p, preferred_element_type=F32)

    def merge(maxes, parts):
        m = functools.reduce(jnp.maximum, maxes)
        r = None
        for mt, part in zip(maxes, parts):
            wpart = jnp.exp2(mt - m) * part
            r = wpart if r is None else r + wpart
        return r[0:HEAD_DIM] * (1.0 / r[HEAD_DIM:HEAD_DIM + 1])

    tiles = [(i, h, t) for grp in range(n_heads // per_group) for i in range(nb)
             for h in range(grp * per_group, (grp + 1) * per_group)
             for t in range((i + 1) * L // KEY_TILE)]
    ctx, maxes, parts, outs, in_flight = {}, {}, {}, {}, []
    for step in range(len(tiles) + PIPE_DEPTH):
        if step < len(tiles):
            i, h, t = tiles[step]
            if t == 0:
                ctx[i, h] = setup(i, h)
                maxes[i, h], parts[i, h] = [], []
            p, mt = score_stage(i, h, t, *ctx[i, h])
            maxes[i, h].append(mt)
            in_flight.append((i, h, t, p))
        if step >= PIPE_DEPTH:
            i, h, t, p = in_flight.pop(0)
            parts[i, h].append(value_stage(h, t, p))
            if t == (i + 1) * L // KEY_TILE - 1:
                outs[i, h] = merge(maxes.pop((i, h)), parts.pop((i, h)))
                if h % per_group == per_group - 1:
                    o_grp = jnp.concatenate(
                        [outs.pop((i, a)) for a in range(h + 1 - per_group, h + 1)], axis=0)
                    o_ref[0, i * L:(i + 1) * L, (h // per_group) * LANES:
                          (h // per_group + 1) * LANES] = o_grp.T.astype(BF16)


def _attention(qT, k, vT, kmean, weights):
    bsz, A, seq = qT.shape
    nb = seq // MOBA_BLOCK
    aw = ATTN_HEADS_PER_STEP * HEAD_DIM
    groups = A // aw
    kern = functools.partial(_attn_kernel, seq=seq, topk=min(MOBA_TOPK, nb - 1),
                             n_heads=ATTN_HEADS_PER_STEP, n_cast=len(weights))
    chan = lambda b, p: (b, p, 0)
    tok = lambda b, p: (b, 0, p)
    slab = lambda b, p: (b * groups + p, 0)
    w_specs = []
    for w in weights:
        rows = w.shape[0] // (bsz * groups)
        assert rows * bsz * groups == w.shape[0] and rows % BF16_ROWS == 0
        w_specs.append(pl.BlockSpec((rows, w.shape[1]), slab))
    res = pl.pallas_call(
        kern,
        grid=(bsz, groups),
        in_specs=[pl.BlockSpec((1, aw, seq), chan),
                  pl.BlockSpec((1, seq, aw), tok),
                  pl.BlockSpec((1, aw, seq), chan),
                  pl.BlockSpec((1, nb, aw), tok)] + w_specs,
        out_specs=[pl.BlockSpec((1, seq, aw), tok)] + w_specs,
        out_shape=[jax.ShapeDtypeStruct((bsz, seq, A), BF16)]
        + [jax.ShapeDtypeStruct(w.shape, BF16) for w in weights],
        compiler_params=pltpu.CompilerParams(
            dimension_semantics=("parallel", "parallel"), vmem_limit_bytes=VMEM_LIMIT_BYTES),
        name="attn",
    )(qT, k, vT, kmean, *weights)
    return res[0], res[1:]


def _post_kernel(x_ref, mod_ref, o_ref, sa_ref, gs_ref, wpa_ref, wout_ref, gffn_ref,
                 wff1_ref, wff2_ref, gfin_ref, out_ref, *, tm, d_ff, final_norm):
    x1s, hs = [], []
    for sub in range(tm // SUB_TILE):
        rows = slice(sub * SUB_TILE, (sub + 1) * SUB_TILE)
        y_attn = jnp.dot(o_ref[0, rows, :], wpa_ref[...], preferred_element_type=F32)
        merged = (sa_ref[0, rows, :].astype(F32) * y_attn
                  + gs_ref[0, rows, :].astype(F32)).astype(BF16)
        x1 = x_ref[0, rows, :] + mod_ref[0, 2:3, :] * jnp.dot(
            merged, wout_ref[...], preferred_element_type=F32)
        xn = (x1 * _rms_scale(x1)) * gffn_ref[...]
        hs.append((xn * (1.0 + mod_ref[0, 4:5, :]) + mod_ref[0, 3:4, :]).astype(BF16))
        x1s.append(x1)
    for sub in range(tm // SUB_TILE):
        acc = None
        for c in range(d_ff // FF_CHUNK):
            cs = slice(c * FF_CHUNK, (c + 1) * FF_CHUNK)
            f = jnp.dot(hs[sub], wff1_ref[:, cs], preferred_element_type=F32)
            f = jnp.square(jnp.maximum(f, 0.0)).astype(BF16)
            part = jnp.dot(f, wff2_ref[cs, :], preferred_element_type=F32)
            acc = part if acc is None else acc + part
        x2 = x1s[sub] + mod_ref[0, 5:6, :] * acc
        if final_norm:
            x2 = (x2 * _rms_scale(x2)) * gfin_ref[...]
        out_ref[0, sub * SUB_TILE:(sub + 1) * SUB_TILE, :] = x2


def _post(x, mod3, o, sa, gs, w_proj_attn, w_out, g_ffn, w_ff1, w_ff2, g_final, *, final_norm):
    bsz, seq, d = x.shape
    tm = POST_TILE
    A = o.shape[-1]
    d_ff = w_ff1.shape[1]
    kern = functools.partial(_post_kernel, tm=tm, d_ff=d_ff, final_norm=final_norm)
    tok = lambda b, t: (b, t, 0)
    return pl.pallas_call(
        kern,
        grid=(bsz, seq // tm),
        in_specs=[pl.BlockSpec((1, tm, d), tok),
                  pl.BlockSpec((1, N_MOD, d), lambda b, t: (b, 0, 0)),
                  pl.BlockSpec((1, tm, A), tok),
                  pl.BlockSpec((1, tm, d), tok),
                  pl.BlockSpec((1, tm, d), tok),
                  _const_spec(w_proj_attn.shape),
                  _const_spec(w_out.shape),
                  _const_spec((1, d)),
                  _const_spec(w_ff1.shape),
                  _const_spec(w_ff2.shape),
                  _const_spec((1, d))],
        out_specs=pl.BlockSpec((1, tm, d), tok),
        out_shape=jax.ShapeDtypeStruct((bsz, seq, d), F32),
        compiler_params=pltpu.CompilerParams(
            dimension_semantics=("parallel", "parallel"), vmem_limit_bytes=VMEM_LIMIT_BYTES),
        name="post",
    )(x, mod3, o, sa, gs, w_proj_attn, w_out, g_ffn.reshape(1, d), w_ff1, w_ff2,
      g_final.reshape(1, d))


def kernel(x, c, w_ada, b_ada, g_mix, w_in, w_proj_attn, g_sgu, w_sgu, b_sgu, w_proj_sgu, w_out,
           g_ffn, w_ff1, w_ff2, g_final):
    bsz, seq, d = x.shape
    depth = w_ada.shape[0]
    A = w_proj_attn.shape[1]
    W = w_proj_sgu.shape[1]
    assert all(seq % t == 0 and t % SUB_TILE == 0 for t in (INPROJ_TILE, POST_TILE))
    assert SUB_TILE % MOBA_BLOCK == 0
    assert SUB_TILE % (SGU_CHUNK * SGU_CHUNKS_PER_DOT) == 0
    assert A % (ATTN_HEADS_PER_STEP * HEAD_DIM) == 0
    assert (ATTN_HEADS_PER_STEP * HEAD_DIM) % LANES == 0
    assert A % LANES == 0 and W % LANES == 0 and LANES % (W // N_SGU_GROUPS) == 0
    assert w_sgu.shape[1:] == (N_SGU_GROUPS, SGU_CHUNK, SGU_CHUNK)
    assert w_in.shape[2] == 3 * A + 2 * W + 2 * d

    for l in range(depth):
        mod3 = _modulation(c, w_ada[l], b_ada[l]).reshape(bsz, N_MOD, d)
        b_full = jnp.repeat(b_sgu[l].T, W // N_SGU_GROUPS, axis=1)
        qT, k, vT, kmean, sa, gs = _inproj(
            x, mod3, g_mix[l], w_in[l].astype(BF16), g_sgu[l], w_sgu[l], b_full,
            w_proj_sgu[l].astype(BF16), attn_w=A, sgu_w=W)
        o, (wpa, wout, wff1, wff2) = _attention(
            qT, k, vT, kmean.reshape(bsz, seq // MOBA_BLOCK, A),
            (w_proj_attn[l], w_out[l], w_ff1[l], w_ff2[l]))
        x = _post(x, mod3, o, sa, gs, wpa, wout, g_ffn[l], wff1, wff2, g_final,
                  final_norm=(l == depth - 1))
    return x
```

```python
import functools
import math

import jax
import jax.numpy as jnp
from jax import lax
from jax.experimental import pallas as pl
from jax.experimental.pallas import tpu as pltpu

HEAD_DIM = 64
MOBA_BLOCK = 256
MOBA_TOPK = 3
N_SGU_GROUPS = 8
SGU_CHUNK = 128
N_MOD = 6
EPS = 1e-6
MASK_BIAS = -1e30
LANES = 128
BF16_ROWS = 16
KEY_TILE = 256
PIPE_DEPTH = 8
ATTN_HEADS_PER_STEP = 8
INPROJ_TILE = 1024
POST_TILE = 1024
SUB_TILE = 256
SGU_CHUNKS_PER_DOT = 2
FF_CHUNK = 1024
MOD_COL_TILE = 1536
VMEM_LIMIT_BYTES = 56 * 1024 * 1024

F32 = jnp.float32
BF16 = jnp.bfloat16


def _const_spec(shape):
    nd = len(shape)
    return pl.BlockSpec(shape, lambda *_: (0,) * nd, pipeline_mode=pl.Buffered(1))


def _rms_scale(x):
    return lax.rsqrt(jnp.mean(x * x, axis=-1, keepdims=True) + EPS)


def _cast_slabs(w_refs, wb_refs):
    for w_ref, wb_ref in zip(w_refs, wb_refs):
        wb_ref[...] = w_ref[...].astype(BF16)


def _slab_specs(weights, n_steps, index_map):
    specs = []
    for w in weights:
        rows = w.shape[0] // n_steps
        assert rows * n_steps == w.shape[0] and rows % BF16_ROWS == 0
        specs.append(pl.BlockSpec((rows, w.shape[1]), index_map))
    return specs


def _mod_kernel(c_ref, w_ref, b_ref, *rest, n_cast):
    _cast_slabs(rest[:n_cast], rest[n_cast + 1:])
    o_ref = rest[n_cast]
    c = c_ref[...]
    c_act = c * jax.nn.sigmoid(c)
    o_ref[...] = jnp.dot(c_act.astype(BF16), w_ref[...].astype(BF16),
                         preferred_element_type=F32) + b_ref[...]


def _modulation(c, w_ada, b_ada, weights):
    bsz, d = c.shape
    n = w_ada.shape[1]
    tn = MOD_COL_TILE
    w_specs = _slab_specs(weights, n // tn, lambda j: (j, 0))
    res = pl.pallas_call(
        functools.partial(_mod_kernel, n_cast=len(weights)),
        grid=(n // tn,),
        in_specs=[pl.BlockSpec((bsz, d), lambda j: (0, 0)),
                  pl.BlockSpec((d, tn), lambda j: (0, j)),
                  pl.BlockSpec((1, tn), lambda j: (0, j))] + w_specs,
        out_specs=[pl.BlockSpec((bsz, tn), lambda j: (0, j))] + w_specs,
        out_shape=[jax.ShapeDtypeStruct((bsz, n), F32)]
        + [jax.ShapeDtypeStruct(w.shape, BF16) for w in weights],
        name="mod",
    )(c, w_ada, b_ada.reshape(1, n), *weights)
    return res[0], res[1:]


def _inproj_kernel(x_ref, mod_ref, gmix_ref, win_ref, gsgu_ref, wsgu_ref, bfull_ref, wps_ref,
                   qT_ref, k_ref, vT_ref, km_ref, sa_ref, gs_ref, su_ref,
                   *, tm, attn_w, sgu_w, d_model, qscale):
    A, W, D = attn_w, sgu_w, d_model
    T = SGU_CHUNK
    c_ga = 3 * A + 2 * W
    row = lax.broadcasted_iota(jnp.int32, (T, T), 0)
    col = lax.broadcasted_iota(jnp.int32, (T, T), 1)
    lane = lax.broadcasted_iota(jnp.int32, (T, LANES), 1)
    gdim = W // N_SGU_GROUPS
    per_vreg = LANES // gdim
    wcat = [jnp.concatenate(
        [jnp.where(row >= col, wsgu_ref[p * per_vreg + a], 0.0) for a in range(per_vreg)],
        axis=1).astype(BF16) for p in range(W // LANES)]

    u, vsn, gate_sgu = [], [], []
    for sub in range(tm // SUB_TILE):
        rows = slice(sub * SUB_TILE, (sub + 1) * SUB_TILE)
        x = x_ref[0, rows, :]
        xn = (x * _rms_scale(x)) * gmix_ref[...]
        h = (xn * (1.0 + mod_ref[0, 1:2, :]) + mod_ref[0, 0:1, :]).astype(BF16)

        def proj(c0, c1):
            return jnp.dot(h, win_ref[:, c0:c1], preferred_element_type=F32)

        u.append(jax.nn.gelu(proj(3 * A, 3 * A + W)))
        vs = jax.nn.gelu(proj(3 * A + W, 3 * A + 2 * W))
        vsn.append((vs * _rms_scale(vs)) * gsgu_ref[...])
        qT_ref[0, :, rows] = (proj(0, A) * qscale).T.astype(BF16)
        k = proj(A, 2 * A)
        k_ref[0, rows, :] = k.astype(BF16)
        for r in range(SUB_TILE // MOBA_BLOCK):
            km_ref[0, sub * (SUB_TILE // MOBA_BLOCK) + r] = jnp.mean(
                k[r * MOBA_BLOCK:(r + 1) * MOBA_BLOCK], axis=0, keepdims=True)
        vT_ref[0, :, rows] = proj(2 * A, 3 * A).T.astype(BF16)
        sa_ref[0, rows, :] = jax.nn.sigmoid(proj(c_ga, c_ga + D)).astype(BF16)
        gate_sgu.append(jax.nn.sigmoid(proj(c_ga + D, c_ga + 2 * D)))

    def stacked(v, c, p):
        vp = v[c * T:(c + 1) * T, p * LANES:(p + 1) * LANES]
        return jnp.concatenate(
            [jnp.where((lane >= a * gdim) & (lane < (a + 1) * gdim), vp, 0.0)
             for a in range(per_vreg)], axis=0).astype(BF16)

    for sub in range(tm // SUB_TILE):
        base = sub * SUB_TILE
        for p in range(W // LANES):
            lanes = slice(p * LANES, (p + 1) * LANES)
            for c in range(0, SUB_TILE // T, SGU_CHUNKS_PER_DOT):
                rhs = jnp.concatenate(
                    [stacked(vsn[sub], c + e, p) for e in range(SGU_CHUNKS_PER_DOT)], axis=1)
                z2 = jnp.dot(wcat[p], rhs, preferred_element_type=F32)
                for e in range(SGU_CHUNKS_PER_DOT):
                    rows = slice((c + e) * T, (c + e + 1) * T)
                    z = z2[:, e * LANES:(e + 1) * LANES] + bfull_ref[:, lanes]
                    su_ref[base + (c + e) * T:base + (c + e + 1) * T, lanes] = (
                        u[sub][rows, lanes] * z).astype(BF16)
        y_sgu = jnp.dot(su_ref[base:base + SUB_TILE, :], wps_ref[...], preferred_element_type=F32)
        gs_ref[0, base:base + SUB_TILE, :] = (gate_sgu[sub] * y_sgu).astype(BF16)


def _inproj(x, mod3, g_mix, w_in, g_sgu, w_sgu, b_full, w_proj_sgu, *, attn_w, sgu_w):
    bsz, seq, d = x.shape
    tm = INPROJ_TILE
    A, W = attn_w, sgu_w
    nb_tile = tm // MOBA_BLOCK
    qscale = (HEAD_DIM ** -0.5) * math.log2(math.e)
    kern = functools.partial(_inproj_kernel, tm=tm, attn_w=A, sgu_w=W, d_model=d, qscale=qscale)
    tok = lambda b, t: (b, t, 0)
    chan = lambda b, t: (b, 0, t)
    return pl.pallas_call(
        kern,
        grid=(bsz, seq // tm),
        in_specs=[pl.BlockSpec((1, tm, d), tok),
                  pl.BlockSpec((1, N_MOD, d), lambda b, t: (b, 0, 0)),
                  _const_spec((1, d)),
                  _const_spec(w_in.shape),
                  _const_spec((1, W)),
                  _const_spec(w_sgu.shape),
                  _const_spec(b_full.shape),
                  _const_spec(w_proj_sgu.shape)],
        out_specs=[pl.BlockSpec((1, A, tm), chan),
                   pl.BlockSpec((1, tm, A), tok),
                   pl.BlockSpec((1, A, tm), chan),
                   pl.BlockSpec((1, nb_tile, 1, A), lambda b, t: (b, t, 0, 0)),
                   pl.BlockSpec((1, tm, d), tok),
                   pl.BlockSpec((1, tm, d), tok)],
        out_shape=[jax.ShapeDtypeStruct((bsz, A, seq), BF16),
                   jax.ShapeDtypeStruct((bsz, seq, A), BF16),
                   jax.ShapeDtypeStruct((bsz, A, seq), BF16),
                   jax.ShapeDtypeStruct((bsz, seq // MOBA_BLOCK, 1, A), F32),
                   jax.ShapeDtypeStruct((bsz, seq, d), BF16),
                   jax.ShapeDtypeStruct((bsz, seq, d), BF16)],
        scratch_shapes=[pltpu.VMEM((tm, W), BF16)],
        compiler_params=pltpu.CompilerParams(
            dimension_semantics=("parallel", "parallel"), vmem_limit_bytes=VMEM_LIMIT_BYTES),
        name="inproj",
    )(x, mod3, g_mix.reshape(1, d), w_in, g_sgu.reshape(1, W), w_sgu, b_full, w_proj_sgu)


def _attn_kernel(qT_ref, k_ref, vT_ref, km_ref, *rest, seq, topk, n_heads, n_cast):
    _cast_slabs(rest[:n_cast], rest[n_cast + 1:])
    o_ref = rest[n_cast]
    L = MOBA_BLOCK
    nb = seq // L
    per_group = LANES // HEAD_DIM
    krow = lax.broadcasted_iota(jnp.int32, (L, L), 0)
    qcol = lax.broadcasted_iota(jnp.int32, (L, L), 1)
    causal_bias = jnp.where(krow <= qcol, 0.0, MASK_BIAS)
    zeros = jnp.zeros((HEAD_DIM, L), BF16)
    ones = jnp.ones((BF16_ROWS, KEY_TILE), BF16)

    def setup(i, h):
        qTh = qT_ref[0, h * HEAD_DIM:(h + 1) * HEAD_DIM, i * L:(i + 1) * L]
        qz = jnp.concatenate(
            [qTh if a == h % per_group else zeros for a in range(per_group)], axis=0)
        bias = [None] * i
        if i > topk:
            lanes = slice(h // per_group * LANES, (h // per_group + 1) * LANES)
            km = km_ref[0, :, 0, lanes].astype(BF16)
            g = jnp.dot(km, qz, preferred_element_type=F32)
            blk = lax.broadcasted_iota(jnp.int32, (nb, L), 0)
            for n in range(i):
                gn = g[n:n + 1, :]
                ahead = jnp.where(blk < n, jnp.where(g >= gn, 1.0, 0.0),
                                  jnp.where(g > gn, 1.0, 0.0))
                ahead = jnp.where(blk < i, ahead, 0.0)
                rank = jnp.sum(ahead, axis=0, keepdims=True)
                bias[n] = jnp.where(rank < topk, 0.0, MASK_BIAS)
        return qz, bias

    def score_stage(i, h, t, qz, bias):
        j = (t * KEY_TILE) // L
        lanes = slice(h // per_group * LANES, (h // per_group + 1) * LANES)
        s = jnp.dot(k_ref[0, t * KEY_TILE:(t + 1) * KEY_TILE, lanes], qz,
                    preferred_element_type=F32)
        if j == i:
            s = s + causal_bias[t * KEY_TILE - i * L:(t + 1) * KEY_TILE - i * L, :]
        sb = s.astype(BF16)
        mb = jnp.max(sb, axis=0, keepdims=True)
        p = jnp.exp2(sb - mb)
        mt = mb.astype(F32)
        return p, (mt + bias[j] if (j < i and bias[j] is not None) else mt)

    def value_stage(h, t, p):
        v_aug = jnp.concatenate(
            [vT_ref[0, h * HEAD_DIM:(h + 1) * HEAD_DIM, t * KEY_TILE:(t + 1) * KEY_TILE], ones],
            axis=0)
        return jnp.dot(v_aug, p, preferred_element_type=F32)

    def merge(maxes, parts):
        m = functools.reduce(jnp.maximum, maxes)
        r = None
        for mt, part in zip(maxes, parts):
            wpart = jnp.exp2(mt - m) * part
            r = wpart if r is None else r + wpart
        return r[0:HEAD_DIM] * (1.0 / r[HEAD_DIM:HEAD_DIM + 1])

    tiles = [(i, h, t) for grp in range(n_heads // per_group) for i in range(nb)
             for h in range(grp * per_group, (grp + 1) * per_group)
             for t in range((i + 1) * L // KEY_TILE)]
    ctx, maxes, parts, outs, in_flight = {}, {}, {}, {}, []
    for step in range(len(tiles) + PIPE_DEPTH):
        if step < len(tiles):
            i, h, t = tiles[step]
            if t == 0:
                ctx[i, h] = setup(i, h)
                maxes[i, h], parts[i, h] = [], []
            p, mt = score_stage(i, h, t, *ctx[i, h])
            maxes[i, h].append(mt)
            in_flight.append((i, h, t, p))
        if step >= PIPE_DEPTH:
            i, h, t, p = in_flight.pop(0)
            parts[i, h].append(value_stage(h, t, p))
            if t == (i + 1) * L // KEY_TILE - 1:
                outs[i, h] = merge(maxes.pop((i, h)), parts.pop((i, h)))
                if h % per_group == per_group - 1:
                    o_grp = jnp.concatenate(
                        [outs.pop((i, a)) for a in range(h + 1 - per_group, h + 1)], axis=0)
                    o_ref[0, i * L:(i + 1) * L, (h // per_group) * LANES:
                          (h // per_group + 1) * LANES] = o_grp.T.astype(BF16)


def _attention(qT, k, vT, kmean, weights):
    bsz, A, seq = qT.shape
    nb = seq // MOBA_BLOCK
    aw = ATTN_HEADS_PER_STEP * HEAD_DIM
    groups = A // aw
    kern = functools.partial(_attn_kernel, seq=seq, topk=min(MOBA_TOPK, nb - 1),
                             n_heads=ATTN_HEADS_PER_STEP, n_cast=len(weights))
    chan = lambda b, p: (b, p, 0)
    tok = lambda b, p: (b, 0, p)
    w_specs = _slab_specs(weights, bsz * groups, lambda b, p: (b * groups + p, 0))
    res = pl.pallas_call(
        kern,
        grid=(bsz, groups),
        in_specs=[pl.BlockSpec((1, aw, seq), chan),
                  pl.BlockSpec((1, seq, aw), tok),
                  pl.BlockSpec((1, aw, seq), chan),
                  pl.BlockSpec((1, nb, 1, aw), lambda b, p: (b, 0, 0, p))] + w_specs,
        out_specs=[pl.BlockSpec((1, seq, aw), tok)] + w_specs,
        out_shape=[jax.ShapeDtypeStruct((bsz, seq, A), BF16)]
        + [jax.ShapeDtypeStruct(w.shape, BF16) for w in weights],
        compiler_params=pltpu.CompilerParams(
            dimension_semantics=("parallel", "parallel"), vmem_limit_bytes=VMEM_LIMIT_BYTES),
        name="attn",
    )(qT, k, vT, kmean, *weights)
    return res[0], res[1:]


def _post_kernel(x_ref, mod_ref, o_ref, sa_ref, gs_ref, wpa_ref, wout_ref, gffn_ref,
                 wff1_ref, wff2_ref, gfin_ref, out_ref, *, tm, d_ff, final_norm):
    x1s, hs = [], []
    for sub in range(tm // SUB_TILE):
        rows = slice(sub * SUB_TILE, (sub + 1) * SUB_TILE)
        y_attn = jnp.dot(o_ref[0, rows, :], wpa_ref[...], preferred_element_type=F32)
        merged = (sa_ref[0, rows, :].astype(F32) * y_attn
                  + gs_ref[0, rows, :].astype(F32)).astype(BF16)
        x1 = x_ref[0, rows, :] + mod_ref[0, 2:3, :] * jnp.dot(
            merged, wout_ref[...], preferred_element_type=F32)
        xn = (x1 * _rms_scale(x1)) * gffn_ref[...]
        hs.append((xn * (1.0 + mod_ref[0, 4:5, :]) + mod_ref[0, 3:4, :]).astype(BF16))
        x1s.append(x1)
    for sub in range(tm // SUB_TILE):
        acc = None
        for c in range(d_ff // FF_CHUNK):
            cs = slice(c * FF_CHUNK, (c + 1) * FF_CHUNK)
            f = jnp.dot(hs[sub], wff1_ref[:, cs], preferred_element_type=F32)
            f = jnp.square(jnp.maximum(f, 0.0)).astype(BF16)
            part = jnp.dot(f, wff2_ref[cs, :], preferred_element_type=F32)
            acc = part if acc is None else acc + part
        x2 = x1s[sub] + mod_ref[0, 5:6, :] * acc
        if final_norm:
            x2 = (x2 * _rms_scale(x2)) * gfin_ref[...]
        out_ref[0, sub * SUB_TILE:(sub + 1) * SUB_TILE, :] = x2


def _post(x, mod3, o, sa, gs, w_proj_attn, w_out, g_ffn, w_ff1, w_ff2, g_final, *, final_norm):
    bsz, seq, d = x.shape
    tm = POST_TILE
    A = o.shape[-1]
    d_ff = w_ff1.shape[1]
    kern = functools.partial(_post_kernel, tm=tm, d_ff=d_ff, final_norm=final_norm)
    tok = lambda b, t: (b, t, 0)
    return pl.pallas_call(
        kern,
        grid=(bsz, seq // tm),
        in_specs=[pl.BlockSpec((1, tm, d), tok),
                  pl.BlockSpec((1, N_MOD, d), lambda b, t: (b, 0, 0)),
                  pl.BlockSpec((1, tm, A), tok),
                  pl.BlockSpec((1, tm, d), tok),
                  pl.BlockSpec((1, tm, d), tok),
                  _const_spec(w_proj_attn.shape),
                  _const_spec(w_out.shape),
                  _const_spec((1, d)),
                  _const_spec(w_ff1.shape),
                  _const_spec(w_ff2.shape),
                  _const_spec((1, d))],
        out_specs=pl.BlockSpec((1, tm, d), tok),
        out_shape=jax.ShapeDtypeStruct((bsz, seq, d), F32),
        compiler_params=pltpu.CompilerParams(
            dimension_semantics=("parallel", "parallel"), vmem_limit_bytes=VMEM_LIMIT_BYTES),
        name="post",
    )(x, mod3, o, sa, gs, w_proj_attn, w_out, g_ffn.reshape(1, d), w_ff1, w_ff2,
      g_final.reshape(1, d))


def kernel(x, c, w_ada, b_ada, g_mix, w_in, w_proj_attn, g_sgu, w_sgu, b_sgu, w_proj_sgu, w_out,
           g_ffn, w_ff1, w_ff2, g_final):
    bsz, seq, d = x.shape
    depth = w_ada.shape[0]
    A = w_proj_attn.shape[1]
    W = w_proj_sgu.shape[1]
    assert all(seq % t == 0 and t % SUB_TILE == 0 for t in (INPROJ_TILE, POST_TILE))
    assert SUB_TILE % MOBA_BLOCK == 0
    assert SUB_TILE % (SGU_CHUNK * SGU_CHUNKS_PER_DOT) == 0
    assert A % (ATTN_HEADS_PER_STEP * HEAD_DIM) == 0
    assert (ATTN_HEADS_PER_STEP * HEAD_DIM) % LANES == 0
    assert A % LANES == 0 and W % LANES == 0 and LANES % (W // N_SGU_GROUPS) == 0
    assert w_sgu.shape[1:] == (N_SGU_GROUPS, SGU_CHUNK, SGU_CHUNK)
    assert w_in.shape[2] == 3 * A + 2 * W + 2 * d

    for l in range(depth):
        mod, (win, wps) = _modulation(c, w_ada[l], b_ada[l], (w_in[l], w_proj_sgu[l]))
        mod3 = mod.reshape(bsz, N_MOD, d)
        b_full = jnp.repeat(b_sgu[l].T, W // N_SGU_GROUPS, axis=1)
        qT, k, vT, kmean, sa, gs = _inproj(
            x, mod3, g_mix[l], win, g_sgu[l], w_sgu[l], b_full, wps, attn_w=A, sgu_w=W)
        o, (wpa, wout, wff1, wff2) = _attention(
            qT, k, vT, kmean,
            (w_proj_attn[l], w_out[l], w_ff1[l], w_ff2[l]))
        x = _post(x, mod3, o, sa, gs, wpa, wout, g_ffn[l], wff1, wff2, g_final,
                  final_norm=(l == depth - 1))
    return x
```

```python
import functools
import math

import jax
import jax.numpy as jnp
from jax import lax
from jax.experimental import pallas as pl
from jax.experimental.pallas import tpu as pltpu

HEAD_DIM = 64
MOBA_BLOCK = 256
MOBA_TOPK = 3
N_SGU_GROUPS = 8
SGU_CHUNK = 128
N_MOD = 6
EPS = 1e-6
MASK_BIAS = -1e30
LANES = 128
BF16_ROWS = 16
KEY_TILE = 256
PIPE_DEPTH = 8
ATTN_HEADS_PER_STEP = 8
INPROJ_TILE = 1024
POST_TILE = 1024
SUB_TILE = 256
SGU_CHUNKS_PER_DOT = 2
FF_CHUNK = 1024
MOD_COL_TILE = 1536
VMEM_LIMIT_BYTES = 56 * 1024 * 1024

F32 = jnp.float32
BF16 = jnp.bfloat16


def _const_spec(shape):
    nd = len(shape)
    return pl.BlockSpec(shape, lambda *_: (0,) * nd, pipeline_mode=pl.Buffered(1))


def _mod_row(mod_ref, k):
    d = mod_ref.shape[-1] // N_MOD
    return mod_ref[0, :, k * d:(k + 1) * d]


def _sigmoid(x):
    return 0.5 * jnp.tanh(0.5 * x) + 0.5


def _rms_scale(x):
    return lax.rsqrt(jnp.mean(x * x, axis=-1, keepdims=True) + EPS)


def _cast_slabs(w_refs, wb_refs):
    for w_ref, wb_ref in zip(w_refs, wb_refs):
        wb_ref[...] = w_ref[...].astype(BF16)


def _slab_specs(weights, n_steps, index_map):
    specs = []
    for w in weights:
        rows = w.shape[0] // n_steps
        assert rows * n_steps == w.shape[0] and rows % BF16_ROWS == 0
        specs.append(pl.BlockSpec((rows, w.shape[1]), index_map))
    return specs


def _mod_kernel(c_ref, w_ref, b_ref, *rest, n_cast):
    _cast_slabs(rest[:n_cast], rest[n_cast + 1:])
    o_ref = rest[n_cast]
    c = c_ref[...]
    c_act = c * jax.nn.sigmoid(c)
    o_ref[:, 0, :] = jnp.dot(c_act.astype(BF16), w_ref[...].astype(BF16),
                             preferred_element_type=F32) + b_ref[...]


def _modulation(c, w_ada, b_ada, weights):
    bsz, d = c.shape
    n = w_ada.shape[1]
    tn = MOD_COL_TILE
    w_specs = _slab_specs(weights, n // tn, lambda j: (j, 0))
    res = pl.pallas_call(
        functools.partial(_mod_kernel, n_cast=len(weights)),
        grid=(n // tn,),
        in_specs=[pl.BlockSpec((bsz, d), lambda j: (0, 0)),
                  pl.BlockSpec((d, tn), lambda j: (0, j)),
                  pl.BlockSpec((1, tn), lambda j: (0, j))] + w_specs,
        out_specs=[pl.BlockSpec((bsz, 1, tn), lambda j: (0, 0, j))] + w_specs,
        out_shape=[jax.ShapeDtypeStruct((bsz, 1, n), F32)]
        + [jax.ShapeDtypeStruct(w.shape, BF16) for w in weights],
        name="mod",
    )(c, w_ada, b_ada.reshape(1, n), *weights)
    return res[0], res[1:]


def _inproj_kernel(x_ref, mod_ref, gmix_ref, win_ref, gsgu_ref, wsgu_ref, bfull_ref, wps_ref,
                   qT_ref, k_ref, vT_ref, km_ref, sa_ref, gs_ref, su_ref,
                   *, tm, attn_w, sgu_w, d_model, qscale):
    A, W, D = attn_w, sgu_w, d_model
    T = SGU_CHUNK
    c_ga = 3 * A + 2 * W
    row = lax.broadcasted_iota(jnp.int32, (T, T), 0)
    col = lax.broadcasted_iota(jnp.int32, (T, T), 1)
    lane = lax.broadcasted_iota(jnp.int32, (T, LANES), 1)
    gdim = W // N_SGU_GROUPS
    per_vreg = LANES // gdim
    wcat = [jnp.concatenate(
        [jnp.where(row >= col, wsgu_ref[p * per_vreg + a], 0.0) for a in range(per_vreg)],
        axis=1).astype(BF16) for p in range(W // LANES)]

    u, vsn, gate_sgu = [], [], []
    for sub in range(tm // SUB_TILE):
        rows = slice(sub * SUB_TILE, (sub + 1) * SUB_TILE)
        x = x_ref[0, rows, :]
        xn = (x * _rms_scale(x)) * gmix_ref[...]
        h = (xn * (1.0 + _mod_row(mod_ref, 1)) + _mod_row(mod_ref, 0)).astype(BF16)

        def proj(c0, c1):
            return jnp.dot(h, win_ref[:, c0:c1], preferred_element_type=F32)

        u.append(jax.nn.gelu(proj(3 * A, 3 * A + W)))
        vs = jax.nn.gelu(proj(3 * A + W, 3 * A + 2 * W))
        vsn.append((vs * _rms_scale(vs)) * gsgu_ref[...])
        qT_ref[0, :, rows] = (proj(0, A) * qscale).T.astype(BF16)
        k = proj(A, 2 * A)
        k_ref[0, rows, :] = k.astype(BF16)
        for r in range(SUB_TILE // MOBA_BLOCK):
            km_ref[0, sub * (SUB_TILE // MOBA_BLOCK) + r] = jnp.mean(
                k[r * MOBA_BLOCK:(r + 1) * MOBA_BLOCK], axis=0, keepdims=True)
        vT_ref[0, :, rows] = proj(2 * A, 3 * A).T.astype(BF16)
        sa_ref[0, rows, :] = _sigmoid(proj(c_ga, c_ga + D)).astype(BF16)
        gate_sgu.append(_sigmoid(proj(c_ga + D, c_ga + 2 * D)))

    def stacked(v, c, p):
        vp = v[c * T:(c + 1) * T, p * LANES:(p + 1) * LANES]
        return jnp.concatenate(
            [jnp.where((lane >= a * gdim) & (lane < (a + 1) * gdim), vp, 0.0)
             for a in range(per_vreg)], axis=0).astype(BF16)

    for sub in range(tm // SUB_TILE):
        base = sub * SUB_TILE
        for p in range(W // LANES):
            lanes = slice(p * LANES, (p + 1) * LANES)
            for c in range(0, SUB_TILE // T, SGU_CHUNKS_PER_DOT):
                rhs = jnp.concatenate(
                    [stacked(vsn[sub], c + e, p) for e in range(SGU_CHUNKS_PER_DOT)], axis=1)
                z2 = jnp.dot(wcat[p], rhs, preferred_element_type=F32)
                for e in range(SGU_CHUNKS_PER_DOT):
                    rows = slice((c + e) * T, (c + e + 1) * T)
                    z = z2[:, e * LANES:(e + 1) * LANES] + bfull_ref[:, lanes]
                    su_ref[base + (c + e) * T:base + (c + e + 1) * T, lanes] = (
                        u[sub][rows, lanes] * z).astype(BF16)
        y_sgu = jnp.dot(su_ref[base:base + SUB_TILE, :], wps_ref[...], preferred_element_type=F32)
        gs_ref[0, base:base + SUB_TILE, :] = (gate_sgu[sub] * y_sgu).astype(BF16)


def _inproj(x, mod3, g_mix, w_in, g_sgu, w_sgu, b_full, w_proj_sgu, *, attn_w, sgu_w):
    bsz, seq, d = x.shape
    tm = INPROJ_TILE
    A, W = attn_w, sgu_w
    nb_tile = tm // MOBA_BLOCK
    qscale = (HEAD_DIM ** -0.5) * math.log2(math.e)
    kern = functools.partial(_inproj_kernel, tm=tm, attn_w=A, sgu_w=W, d_model=d, qscale=qscale)
    tok = lambda b, t: (b, t, 0)
    chan = lambda b, t: (b, 0, t)
    return pl.pallas_call(
        kern,
        grid=(bsz, seq // tm),
        in_specs=[pl.BlockSpec((1, tm, d), tok),
                  pl.BlockSpec((1, 1, N_MOD * d), lambda b, t: (b, 0, 0)),
                  _const_spec((1, d)),
                  _const_spec(w_in.shape),
                  _const_spec((1, W)),
                  _const_spec(w_sgu.shape),
                  _const_spec(b_full.shape),
                  _const_spec(w_proj_sgu.shape)],
        out_specs=[pl.BlockSpec((1, A, tm), chan),
                   pl.BlockSpec((1, tm, A), tok),
                   pl.BlockSpec((1, A, tm), chan),
                   pl.BlockSpec((1, nb_tile, 1, A), lambda b, t: (b, t, 0, 0)),
                   pl.BlockSpec((1, tm, d), tok),
                   pl.BlockSpec((1, tm, d), tok)],
        out_shape=[jax.ShapeDtypeStruct((bsz, A, seq), BF16),
                   jax.ShapeDtypeStruct((bsz, seq, A), BF16),
                   jax.ShapeDtypeStruct((bsz, A, seq), BF16),
                   jax.ShapeDtypeStruct((bsz, seq // MOBA_BLOCK, 1, A), F32),
                   jax.ShapeDtypeStruct((bsz, seq, d), BF16),
                   jax.ShapeDtypeStruct((bsz, seq, d), BF16)],
        scratch_shapes=[pltpu.VMEM((tm, W), BF16)],
        compiler_params=pltpu.CompilerParams(
            dimension_semantics=("parallel", "parallel"), vmem_limit_bytes=VMEM_LIMIT_BYTES),
        name="inproj",
    )(x, mod3, g_mix.reshape(1, d), w_in, g_sgu.reshape(1, W), w_sgu, b_full, w_proj_sgu)


def _attn_kernel(qT_ref, k_ref, vT_ref, km_ref, *rest, seq, topk, n_heads, n_cast):
    _cast_slabs(rest[:n_cast], rest[n_cast + 1:])
    o_ref = rest[n_cast]
    L = MOBA_BLOCK
    nb = seq // L
    per_group = LANES // HEAD_DIM
    krow = lax.broadcasted_iota(jnp.int32, (L, L), 0)
    qcol = lax.broadcasted_iota(jnp.int32, (L, L), 1)
    causal_bias = jnp.where(krow <= qcol, 0.0, MASK_BIAS)
    zeros = jnp.zeros((HEAD_DIM, L), BF16)
    ones = jnp.ones((BF16_ROWS, KEY_TILE), BF16)

    def setup(i, h):
        qTh = qT_ref[0, h * HEAD_DIM:(h + 1) * HEAD_DIM, i * L:(i + 1) * L]
        qz = jnp.concatenate(
            [qTh if a == h % per_group else zeros for a in range(per_group)], axis=0)
        bias = [None] * i
        if i > topk:
            lanes = slice(h // per_group * LANES, (h // per_group + 1) * LANES)
            km = km_ref[0, :, 0, lanes].astype(BF16)
            g = jnp.dot(km, qz, preferred_element_type=F32)
            blk = lax.broadcasted_iota(jnp.int32, (nb, L), 0)
            for n in range(i):
                gn = g[n:n + 1, :]
                ahead = jnp.where(blk < n, jnp.where(g >= gn, 1.0, 0.0),
                                  jnp.where(g > gn, 1.0, 0.0))
                ahead = jnp.where(blk < i, ahead, 0.0)
                rank = jnp.sum(ahead, axis=0, keepdims=True)
                bias[n] = jnp.where(rank < topk, 0.0, MASK_BIAS)
        return qz, bias

    def score_stage(i, h, t, qz, bias):
        j = (t * KEY_TILE) // L
        lanes = slice(h // per_group * LANES, (h // per_group + 1) * LANES)
        s = jnp.dot(k_ref[0, t * KEY_TILE:(t + 1) * KEY_TILE, lanes], qz,
                    preferred_element_type=F32)
        if j == i:
            s = s + causal_bias[t * KEY_TILE - i * L:(t + 1) * KEY_TILE - i * L, :]
        sb = s.astype(BF16)
        mb = jnp.max(sb, axis=0, keepdims=True)
        p = jnp.exp2(sb - mb)
        mt = mb.astype(F32)
        return p, (mt + bias[j] if (j < i and bias[j] is not None) else mt)

    def value_stage(h, t, p):
        v_aug = jnp.concatenate(
            [vT_ref[0, h * HEAD_DIM:(h + 1) * HEAD_DIM, t * KEY_TILE:(t + 1) * KEY_TILE], ones],
            axis=0)
        return jnp.dot(v_aug, p, preferred_element_type=F32)

    def merge(maxes, parts):
        m = functools.reduce(jnp.maximum, maxes)
        r = None
        for mt, part in zip(maxes, parts):
            wpart = jnp.exp2(mt - m) * part
            r = wpart if r is None else r + wpart
        return r[0:HEAD_DIM] * (1.0 / r[HEAD_DIM:HEAD_DIM + 1])

    tiles = [(i, h, t) for grp in range(n_heads // per_group) for i in range(nb)
             for h in range(grp * per_group, (grp + 1) * per_group)
             for t in range((i + 1) * L // KEY_TILE)]
    ctx, maxes, parts, outs, in_flight = {}, {}, {}, {}, []
    for step in range(len(tiles) + PIPE_DEPTH):
        if step < len(tiles):
            i, h, t = tiles[step]
            if t == 0:
                ctx[i, h] = setup(i, h)
                maxes[i, h], parts[i, h] = [], []
            p, mt = score_stage(i, h, t, *ctx[i, h])
            maxes[i, h].append(mt)
            in_flight.append((i, h, t, p))
        if step >= PIPE_DEPTH:
            i, h, t, p = in_flight.pop(0)
            parts[i, h].append(value_stage(h, t, p))
            if t == (i + 1) * L // KEY_TILE - 1:
                outs[i, h] = merge(maxes.pop((i, h)), parts.pop((i, h)))
                if h % per_group == per_group - 1:
                    o_grp = jnp.concatenate(
                        [outs.pop((i, a)) for a in range(h + 1 - per_group, h + 1)], axis=0)
                    o_ref[0, i * L:(i + 1) * L, (h // per_group) * LANES:
                          (h // per_group + 1) * LANES] = o_grp.T.astype(BF16)


def _attention(qT, k, vT, kmean, weights):
    bsz, A, seq = qT.shape
    nb = seq // MOBA_BLOCK
    aw = ATTN_HEADS_PER_STEP * HEAD_DIM
    groups = A // aw
    kern = functools.partial(_attn_kernel, seq=seq, topk=min(MOBA_TOPK, nb - 1),
                             n_heads=ATTN_HEADS_PER_STEP, n_cast=len(weights))
    chan = lambda b, p: (b, p, 0)
    tok = lambda b, p: (b, 0, p)
    w_specs = _slab_specs(weights, bsz * groups, lambda b, p: (b * groups + p, 0))
    res = pl.pallas_call(
        kern,
        grid=(bsz, groups),
        in_specs=[pl.BlockSpec((1, aw, seq), chan),
                  pl.BlockSpec((1, seq, aw), tok),
                  pl.BlockSpec((1, aw, seq), chan),
                  pl.BlockSpec((1, nb, 1, aw), lambda b, p: (b, 0, 0, p))] + w_specs,
        out_specs=[pl.BlockSpec((1, seq, aw), tok)] + w_specs,
        out_shape=[jax.ShapeDtypeStruct((bsz, seq, A), BF16)]
        + [jax.ShapeDtypeStruct(w.shape, BF16) for w in weights],
        compiler_params=pltpu.CompilerParams(
            dimension_semantics=("parallel", "parallel"), vmem_limit_bytes=VMEM_LIMIT_BYTES),
        name="attn",
    )(qT, k, vT, kmean, *weights)
    return res[0], res[1:]


def _post_kernel(x_ref, mod_ref, o_ref, sa_ref, gs_ref, wpa_ref, wout_ref, gffn_ref,
                 wff1_ref, wff2_ref, gfin_ref, out_ref, *, tm, d_ff, final_norm):
    x1s, hs = [], []
    for sub in range(tm // SUB_TILE):
        rows = slice(sub * SUB_TILE, (sub + 1) * SUB_TILE)
        y_attn = jnp.dot(o_ref[0, rows, :], wpa_ref[...], preferred_element_type=F32)
        merged = (sa_ref[0, rows, :].astype(F32) * y_attn
                  + gs_ref[0, rows, :].astype(F32)).astype(BF16)
        x1 = x_ref[0, rows, :] + _mod_row(mod_ref, 2) * jnp.dot(
            merged, wout_ref[...], preferred_element_type=F32)
        xn = (x1 * _rms_scale(x1)) * gffn_ref[...]
        hs.append((xn * (1.0 + _mod_row(mod_ref, 4)) + _mod_row(mod_ref, 3)).astype(BF16))
        x1s.append(x1)
    for sub in range(tm // SUB_TILE):
        acc = None
        for c in range(d_ff // FF_CHUNK):
            cs = slice(c * FF_CHUNK, (c + 1) * FF_CHUNK)
            f = jnp.dot(hs[sub], wff1_ref[:, cs], preferred_element_type=F32)
            f = jnp.square(jnp.maximum(f, 0.0)).astype(BF16)
            part = jnp.dot(f, wff2_ref[cs, :], preferred_element_type=F32)
            acc = part if acc is None else acc + part
        x2 = x1s[sub] + _mod_row(mod_ref, 5) * acc
        if final_norm:
            x2 = (x2 * _rms_scale(x2)) * gfin_ref[...]
        out_ref[0, sub * SUB_TILE:(sub + 1) * SUB_TILE, :] = x2


def _post(x, mod3, o, sa, gs, w_proj_attn, w_out, g_ffn, w_ff1, w_ff2, g_final, *, final_norm):
    bsz, seq, d = x.shape
    tm = POST_TILE
    A = o.shape[-1]
    d_ff = w_ff1.shape[1]
    kern = functools.partial(_post_kernel, tm=tm, d_ff=d_ff, final_norm=final_norm)
    tok = lambda b, t: (b, t, 0)
    return pl.pallas_call(
        kern,
        grid=(bsz, seq // tm),
        in_specs=[pl.BlockSpec((1, tm, d), tok),
                  pl.BlockSpec((1, 1, N_MOD * d), lambda b, t: (b, 0, 0)),
                  pl.BlockSpec((1, tm, A), tok),
                  pl.BlockSpec((1, tm, d), tok),
                  pl.BlockSpec((1, tm, d), tok),
                  _const_spec(w_proj_attn.shape),
                  _const_spec(w_out.shape),
                  _const_spec((1, d)),
                  _const_spec(w_ff1.shape),
                  _const_spec(w_ff2.shape),
                  _const_spec((1, d))],
        out_specs=pl.BlockSpec((1, tm, d), tok),
        out_shape=jax.ShapeDtypeStruct((bsz, seq, d), F32),
        compiler_params=pltpu.CompilerParams(
            dimension_semantics=("parallel", "parallel"), vmem_limit_bytes=VMEM_LIMIT_BYTES),
        name="post",
    )(x, mod3, o, sa, gs, w_proj_attn, w_out, g_ffn.reshape(1, d), w_ff1, w_ff2,
      g_final.reshape(1, d))


def kernel(x, c, w_ada, b_ada, g_mix, w_in, w_proj_attn, g_sgu, w_sgu, b_sgu, w_proj_sgu, w_out,
           g_ffn, w_ff1, w_ff2, g_final):
    bsz, seq, d = x.shape
    depth = w_ada.shape[0]
    A = w_proj_attn.shape[1]
    W = w_proj_sgu.shape[1]
    assert all(seq % t == 0 and t % SUB_TILE == 0 for t in (INPROJ_TILE, POST_TILE))
    assert SUB_TILE % MOBA_BLOCK == 0
    assert SUB_TILE % (SGU_CHUNK * SGU_CHUNKS_PER_DOT) == 0
    assert A % (ATTN_HEADS_PER_STEP * HEAD_DIM) == 0
    assert (ATTN_HEADS_PER_STEP * HEAD_DIM) % LANES == 0
    assert A % LANES == 0 and W % LANES == 0 and LANES % (W // N_SGU_GROUPS) == 0
    assert w_sgu.shape[1:] == (N_SGU_GROUPS, SGU_CHUNK, SGU_CHUNK)
    assert w_in.shape[2] == 3 * A + 2 * W + 2 * d

    for l in range(depth):
        mod3, (win, wps) = _modulation(c, w_ada[l], b_ada[l], (w_in[l], w_proj_sgu[l]))
        b_full = jnp.repeat(b_sgu[l].T, W // N_SGU_GROUPS, axis=1)
        qT, k, vT, kmean, sa, gs = _inproj(
            x, mod3, g_mix[l], win, g_sgu[l], w_sgu[l], b_full, wps, attn_w=A, sgu_w=W)
        o, (wpa, wout, wff1, wff2) = _attention(
            qT, k, vT, kmean,
            (w_proj_attn[l], w_out[l], w_ff1[l], w_ff2[l]))
        x = _post(x, mod3, o, sa, gs, wpa, wout, g_ffn[l], wff1, wff2, g_final,
                  final_norm=(l == depth - 1))
    return x
```

```python
import functools
import math

import jax
import jax.numpy as jnp
from jax import lax
from jax.experimental import pallas as pl
from jax.experimental.pallas import tpu as pltpu

HEAD_DIM = 64
MOBA_BLOCK = 256
MOBA_TOPK = 3
N_SGU_GROUPS = 8
SGU_CHUNK = 128
N_MOD = 6
EPS = 1e-6
MASK_BIAS = -1e30
LANES = 128
BF16_ROWS = 16
KEY_TILE = 256
PIPE_DEPTH = 8
ATTN_HEADS_PER_STEP = 8
INPROJ_TILE = 1024
POST_TILE = 1024
SUB_TILE = 256
SGU_CHUNKS_PER_DOT = 2
FF_CHUNK = 1024
MOD_COL_TILE = 1536
VMEM_LIMIT_BYTES = 56 * 1024 * 1024

F32 = jnp.float32
BF16 = jnp.bfloat16


def _const_spec(shape):
    nd = len(shape)
    return pl.BlockSpec(shape, lambda *_: (0,) * nd, pipeline_mode=pl.Buffered(1))


def _mod_row(mod_ref, k):
    d = mod_ref.shape[-1] // N_MOD
    return mod_ref[0, :, k * d:(k + 1) * d]


def _sigmoid(x):
    return 0.5 * jnp.tanh(0.5 * x) + 0.5


def _rms_scale(x):
    return lax.rsqrt(jnp.mean(x * x, axis=-1, keepdims=True) + EPS)


def _cast_slabs(w_refs, wb_refs):
    for w_ref, wb_ref in zip(w_refs, wb_refs):
        wb_ref[...] = w_ref[...].astype(BF16)


def _slab_specs(weights, n_steps, index_map):
    specs = []
    for w in weights:
        rows = w.shape[0] // n_steps
        assert rows * n_steps == w.shape[0] and rows % BF16_ROWS == 0
        specs.append(pl.BlockSpec((rows, w.shape[1]), index_map))
    return specs


def _mod_kernel(c_ref, w_ref, b_ref, *rest, n_cast):
    _cast_slabs(rest[:n_cast], rest[n_cast + 1:])
    o_ref = rest[n_cast]
    c = c_ref[...]
    c_act = c * jax.nn.sigmoid(c)
    o_ref[:, 0, :] = jnp.dot(c_act.astype(BF16), w_ref[...].astype(BF16),
                             preferred_element_type=F32) + b_ref[...]


def _modulation(c, w_ada, b_ada, weights):
    bsz, d = c.shape
    n = w_ada.shape[1]
    tn = MOD_COL_TILE
    w_specs = _slab_specs(weights, n // tn, lambda j: (j, 0))
    res = pl.pallas_call(
        functools.partial(_mod_kernel, n_cast=len(weights)),
        grid=(n // tn,),
        in_specs=[pl.BlockSpec((bsz, d), lambda j: (0, 0)),
                  pl.BlockSpec((d, tn), lambda j: (0, j)),
                  pl.BlockSpec((1, tn), lambda j: (0, j))] + w_specs,
        out_specs=[pl.BlockSpec((bsz, 1, tn), lambda j: (0, 0, j))] + w_specs,
        out_shape=[jax.ShapeDtypeStruct((bsz, 1, n), F32)]
        + [jax.ShapeDtypeStruct(w.shape, BF16) for w in weights],
        name="mod",
    )(c, w_ada, b_ada.reshape(1, n), *weights)
    return res[0], res[1:]


def _inproj_kernel(x_ref, mod_ref, gmix_ref, win_ref, gsgu_ref, wsgu_ref, bfull_ref, wps_ref,
                   qT_ref, k_ref, vT_ref, km_ref, sa_ref, gs_ref, su_ref,
                   *, tm, attn_w, sgu_w, d_model, qscale):
    A, W, D = attn_w, sgu_w, d_model
    T = SGU_CHUNK
    c_ga = 3 * A + 2 * W
    row = lax.broadcasted_iota(jnp.int32, (T, T), 0)
    col = lax.broadcasted_iota(jnp.int32, (T, T), 1)
    lane = lax.broadcasted_iota(jnp.int32, (T, LANES), 1)
    gdim = W // N_SGU_GROUPS
    per_vreg = LANES // gdim
    wcat = [jnp.concatenate(
        [jnp.where(row >= col, wsgu_ref[p * per_vreg + a], 0.0) for a in range(per_vreg)],
        axis=1).astype(BF16) for p in range(W // LANES)]

    u, vsn, gate_sgu = [], [], []

    def main_stage(sub, between):
        rows = slice(sub * SUB_TILE, (sub + 1) * SUB_TILE)
        x = x_ref[0, rows, :]
        xn = (x * _rms_scale(x)) * gmix_ref[...]
        h = (xn * (1.0 + _mod_row(mod_ref, 1)) + _mod_row(mod_ref, 0)).astype(BF16)
        between = list(between)

        def proj(c0, c1):
            out = jnp.dot(h, win_ref[:, c0:c1], preferred_element_type=F32)
            if between:
                between.pop(0)()
            return out

        u.append(jax.nn.gelu(proj(3 * A, 3 * A + W)))
        vs = jax.nn.gelu(proj(3 * A + W, 3 * A + 2 * W))
        vsn.append((vs * _rms_scale(vs)) * gsgu_ref[...])
        qT_ref[0, :, rows] = (proj(0, A) * qscale).T.astype(BF16)
        k = proj(A, 2 * A)
        k_ref[0, rows, :] = k.astype(BF16)
        for r in range(SUB_TILE // MOBA_BLOCK):
            km_ref[0, sub * (SUB_TILE // MOBA_BLOCK) + r] = jnp.mean(
                k[r * MOBA_BLOCK:(r + 1) * MOBA_BLOCK], axis=0, keepdims=True)
        vT_ref[0, :, rows] = proj(2 * A, 3 * A).T.astype(BF16)
        sa_ref[0, rows, :] = _sigmoid(proj(c_ga, c_ga + D)).astype(BF16)
        gate_sgu.append(_sigmoid(proj(c_ga + D, c_ga + 2 * D)))
        assert not between

    def stacked(v, c, p):
        vp = v[c * T:(c + 1) * T, p * LANES:(p + 1) * LANES]
        return jnp.concatenate(
            [jnp.where((lane >= a * gdim) & (lane < (a + 1) * gdim), vp, 0.0)
             for a in range(per_vreg)], axis=0).astype(BF16)

    def spatial_mix(sub, p):
        base = sub * SUB_TILE
        lanes = slice(p * LANES, (p + 1) * LANES)
        for c in range(0, SUB_TILE // T, SGU_CHUNKS_PER_DOT):
            rhs = jnp.concatenate(
                [stacked(vsn[sub], c + e, p) for e in range(SGU_CHUNKS_PER_DOT)], axis=1)
            z2 = jnp.dot(wcat[p], rhs, preferred_element_type=F32)
            for e in range(SGU_CHUNKS_PER_DOT):
                rows = slice((c + e) * T, (c + e + 1) * T)
                z = z2[:, e * LANES:(e + 1) * LANES] + bfull_ref[:, lanes]
                su_ref[base + (c + e) * T:base + (c + e + 1) * T, lanes] = (
                    u[sub][rows, lanes] * z).astype(BF16)

    def sgu_out(sub):
        base = sub * SUB_TILE
        y_sgu = jnp.dot(su_ref[base:base + SUB_TILE, :], wps_ref[...], preferred_element_type=F32)
        gs_ref[0, base:base + SUB_TILE, :] = (gate_sgu[sub] * y_sgu).astype(BF16)

    def sgu_steps(sub):
        return [functools.partial(spatial_mix, sub, p) for p in range(W // LANES)] + [
            functools.partial(sgu_out, sub)]

    n_sub = tm // SUB_TILE
    for sub in range(n_sub):
        main_stage(sub, sgu_steps(sub - 1) if sub else [])
    for step in sgu_steps(n_sub - 1):
        step()


def _inproj(x, mod3, g_mix, w_in, g_sgu, w_sgu, b_full, w_proj_sgu, *, attn_w, sgu_w):
    bsz, seq, d = x.shape
    tm = INPROJ_TILE
    A, W = attn_w, sgu_w
    nb_tile = tm // MOBA_BLOCK
    qscale = (HEAD_DIM ** -0.5) * math.log2(math.e)
    kern = functools.partial(_inproj_kernel, tm=tm, attn_w=A, sgu_w=W, d_model=d, qscale=qscale)
    tok = lambda b, t: (b, t, 0)
    chan = lambda b, t: (b, 0, t)
    return pl.pallas_call(
        kern,
        grid=(bsz, seq // tm),
        in_specs=[pl.BlockSpec((1, tm, d), tok),
                  pl.BlockSpec((1, 1, N_MOD * d), lambda b, t: (b, 0, 0)),
                  _const_spec((1, d)),
                  _const_spec(w_in.shape),
                  _const_spec((1, W)),
                  _const_spec(w_sgu.shape),
                  _const_spec(b_full.shape),
                  _const_spec(w_proj_sgu.shape)],
        out_specs=[pl.BlockSpec((1, A, tm), chan),
                   pl.BlockSpec((1, tm, A), tok),
                   pl.BlockSpec((1, A, tm), chan),
                   pl.BlockSpec((1, nb_tile, 1, A), lambda b, t: (b, t, 0, 0)),
                   pl.BlockSpec((1, tm, d), tok),
                   pl.BlockSpec((1, tm, d), tok)],
        out_shape=[jax.ShapeDtypeStruct((bsz, A, seq), BF16),
                   jax.ShapeDtypeStruct((bsz, seq, A), BF16),
                   jax.ShapeDtypeStruct((bsz, A, seq), BF16),
                   jax.ShapeDtypeStruct((bsz, seq // MOBA_BLOCK, 1, A), F32),
                   jax.ShapeDtypeStruct((bsz, seq, d), BF16),
                   jax.ShapeDtypeStruct((bsz, seq, d), BF16)],
        scratch_shapes=[pltpu.VMEM((tm, W), BF16)],
        compiler_params=pltpu.CompilerParams(
            dimension_semantics=("parallel", "parallel"), vmem_limit_bytes=VMEM_LIMIT_BYTES),
        name="inproj",
    )(x, mod3, g_mix.reshape(1, d), w_in, g_sgu.reshape(1, W), w_sgu, b_full, w_proj_sgu)


def _attn_kernel(qT_ref, k_ref, vT_ref, km_ref, *rest, seq, topk, n_heads, n_cast):
    _cast_slabs(rest[:n_cast], rest[n_cast + 1:])
    o_ref = rest[n_cast]
    L = MOBA_BLOCK
    nb = seq // L
    per_group = LANES // HEAD_DIM
    krow = lax.broadcasted_iota(jnp.int32, (L, L), 0)
    qcol = lax.broadcasted_iota(jnp.int32, (L, L), 1)
    causal_bias = jnp.where(krow <= qcol, 0.0, MASK_BIAS)
    zeros = jnp.zeros((HEAD_DIM, L), BF16)
    ones = jnp.ones((BF16_ROWS, KEY_TILE), BF16)

    def setup(i, h):
        qTh = qT_ref[0, h * HEAD_DIM:(h + 1) * HEAD_DIM, i * L:(i + 1) * L]
        qz = jnp.concatenate(
            [qTh if a == h % per_group else zeros for a in range(per_group)], axis=0)
        bias = [None] * i
        if i > topk:
            lanes = slice(h // per_group * LANES, (h // per_group + 1) * LANES)
            km = km_ref[0, :, 0, lanes].astype(BF16)
            g = jnp.dot(km, qz, preferred_element_type=F32)
            blk = lax.broadcasted_iota(jnp.int32, (nb, L), 0)
            for n in range(i):
                gn = g[n:n + 1, :]
                ahead = jnp.where(blk < n, jnp.where(g >= gn, 1.0, 0.0),
                                  jnp.where(g > gn, 1.0, 0.0))
                ahead = jnp.where(blk < i, ahead, 0.0)
                rank = jnp.sum(ahead, axis=0, keepdims=True)
                bias[n] = jnp.where(rank < topk, 0.0, MASK_BIAS)
        return qz, bias

    def score_stage(i, h, t, qz, bias):
        j = (t * KEY_TILE) // L
        lanes = slice(h // per_group * LANES, (h // per_group + 1) * LANES)
        s = jnp.dot(k_ref[0, t * KEY_TILE:(t + 1) * KEY_TILE, lanes], qz,
                    preferred_element_type=F32)
        if j == i:
            s = s + causal_bias[t * KEY_TILE - i * L:(t + 1) * KEY_TILE - i * L, :]
        sb = s.astype(BF16)
        mb = jnp.max(sb, axis=0, keepdims=True)
        p = jnp.exp2(sb - mb)
        mt = mb.astype(F32)
        return p, (mt + bias[j] if (j < i and bias[j] is not None) else mt)

    def value_stage(h, t, p):
        v_aug = jnp.concatenate(
            [vT_ref[0, h * HEAD_DIM:(h + 1) * HEAD_DIM, t * KEY_TILE:(t + 1) * KEY_TILE], ones],
            axis=0)
        return jnp.dot(v_aug, p, preferred_element_type=F32)

    def merge(maxes, parts):
        m = functools.reduce(jnp.maximum, maxes)
        r = None
        for mt, part in zip(maxes, parts):
            wpart = jnp.exp2(mt - m) * part
            r = wpart if r is None else r + wpart
        return r[0:HEAD_DIM] * (1.0 / r[HEAD_DIM:HEAD_DIM + 1])

    tiles = [(i, h, t) for grp in range(n_heads // per_group) for i in range(nb)
             for h in range(grp * per_group, (grp + 1) * per_group)
             for t in range((i + 1) * L // KEY_TILE)]
    ctx, maxes, parts, outs, in_flight = {}, {}, {}, {}, []
    for step in range(len(tiles) + PIPE_DEPTH):
        if step < len(tiles):
            i, h, t = tiles[step]
            if t == 0:
                ctx[i, h] = setup(i, h)
                maxes[i, h], parts[i, h] = [], []
            p, mt = score_stage(i, h, t, *ctx[i, h])
            maxes[i, h].append(mt)
            in_flight.append((i, h, t, p))
        if step >= PIPE_DEPTH:
            i, h, t, p = in_flight.pop(0)
            parts[i, h].append(value_stage(h, t, p))
            if t == (i + 1) * L // KEY_TILE - 1:
                outs[i, h] = merge(maxes.pop((i, h)), parts.pop((i, h)))
                if h % per_group == per_group - 1:
                    o_grp = jnp.concatenate(
                        [outs.pop((i, a)) for a in range(h + 1 - per_group, h + 1)], axis=0)
                    o_ref[0, i * L:(i + 1) * L, (h // per_group) * LANES:
                          (h // per_group + 1) * LANES] = o_grp.T.astype(BF16)


def _attention(qT, k, vT, kmean, weights):
    bsz, A, seq = qT.shape
    nb = seq // MOBA_BLOCK
    aw = ATTN_HEADS_PER_STEP * HEAD_DIM
    groups = A // aw
    kern = functools.partial(_attn_kernel, seq=seq, topk=min(MOBA_TOPK, nb - 1),
                             n_heads=ATTN_HEADS_PER_STEP, n_cast=len(weights))
    chan = lambda b, p: (b, p, 0)
    tok = lambda b, p: (b, 0, p)
    w_specs = _slab_specs(weights, bsz * groups, lambda b, p: (b * groups + p, 0))
    res = pl.pallas_call(
        kern,
        grid=(bsz, groups),
        in_specs=[pl.BlockSpec((1, aw, seq), chan),
                  pl.BlockSpec((1, seq, aw), tok),
                  pl.BlockSpec((1, aw, seq), chan),
                  pl.BlockSpec((1, nb, 1, aw), lambda b, p: (b, 0, 0, p))] + w_specs,
        out_specs=[pl.BlockSpec((1, seq, aw), tok)] + w_specs,
        out_shape=[jax.ShapeDtypeStruct((bsz, seq, A), BF16)]
        + [jax.ShapeDtypeStruct(w.shape, BF16) for w in weights],
        compiler_params=pltpu.CompilerParams(
            dimension_semantics=("parallel", "parallel"), vmem_limit_bytes=VMEM_LIMIT_BYTES),
        name="attn",
    )(qT, k, vT, kmean, *weights)
    return res[0], res[1:]


def _post_kernel(x_ref, mod_ref, o_ref, sa_ref, gs_ref, wpa_ref, wout_ref, gffn_ref,
                 wff1_ref, wff2_ref, gfin_ref, out_ref, *, tm, d_ff, final_norm):
    x1s, hs = [], []
    for sub in range(tm // SUB_TILE):
        rows = slice(sub * SUB_TILE, (sub + 1) * SUB_TILE)
        y_attn = jnp.dot(o_ref[0, rows, :], wpa_ref[...], preferred_element_type=F32)
        merged = (sa_ref[0, rows, :].astype(F32) * y_attn
                  + gs_ref[0, rows, :].astype(F32)).astype(BF16)
        x1 = x_ref[0, rows, :] + _mod_row(mod_ref, 2) * jnp.dot(
            merged, wout_ref[...], preferred_element_type=F32)
        xn = (x1 * _rms_scale(x1)) * gffn_ref[...]
        hs.append((xn * (1.0 + _mod_row(mod_ref, 4)) + _mod_row(mod_ref, 3)).astype(BF16))
        x1s.append(x1)
    for sub in range(tm // SUB_TILE):
        acc = None
        for c in range(d_ff // FF_CHUNK):
            cs = slice(c * FF_CHUNK, (c + 1) * FF_CHUNK)
            f = jnp.dot(hs[sub], wff1_ref[:, cs], preferred_element_type=F32)
            f = jnp.square(jnp.maximum(f, 0.0)).astype(BF16)
            part = jnp.dot(f, wff2_ref[cs, :], preferred_element_type=F32)
            acc = part if acc is None else acc + part
        x2 = x1s[sub] + _mod_row(mod_ref, 5) * acc
        if final_norm:
            x2 = (x2 * _rms_scale(x2)) * gfin_ref[...]
        out_ref[0, sub * SUB_TILE:(sub + 1) * SUB_TILE, :] = x2


def _post(x, mod3, o, sa, gs, w_proj_attn, w_out, g_ffn, w_ff1, w_ff2, g_final, *, final_norm):
    bsz, seq, d = x.shape
    tm = POST_TILE
    A = o.shape[-1]
    d_ff = w_ff1.shape[1]
    kern = functools.partial(_post_kernel, tm=tm, d_ff=d_ff, final_norm=final_norm)
    tok = lambda b, t: (b, t, 0)
    return pl.pallas_call(
        kern,
        grid=(bsz, seq // tm),
        in_specs=[pl.BlockSpec((1, tm, d), tok),
                  pl.BlockSpec((1, 1, N_MOD * d), lambda b, t: (b, 0, 0)),
                  pl.BlockSpec((1, tm, A), tok),
                  pl.BlockSpec((1, tm, d), tok),
                  pl.BlockSpec((1, tm, d), tok),
                  _const_spec(w_proj_attn.shape),
                  _const_spec(w_out.shape),
                  _const_spec((1, d)),
                  _const_spec(w_ff1.shape),
                  _const_spec(w_ff2.shape),
                  _const_spec((1, d))],
        out_specs=pl.BlockSpec((1, tm, d), tok),
        out_shape=jax.ShapeDtypeStruct((bsz, seq, d), F32),
        compiler_params=pltpu.CompilerParams(
            dimension_semantics=("parallel", "parallel"), vmem_limit_bytes=VMEM_LIMIT_BYTES),
        name="post",
    )(x, mod3, o, sa, gs, w_proj_attn, w_out, g_ffn.reshape(1, d), w_ff1, w_ff2,
      g_final.reshape(1, d))


def kernel(x, c, w_ada, b_ada, g_mix, w_in, w_proj_attn, g_sgu, w_sgu, b_sgu, w_proj_sgu, w_out,
           g_ffn, w_ff1, w_ff2, g_final):
    bsz, seq, d = x.shape
    depth = w_ada.shape[0]
    A = w_proj_attn.shape[1]
    W = w_proj_sgu.shape[1]
    assert all(seq % t == 0 and t % SUB_TILE == 0 for t in (INPROJ_TILE, POST_TILE))
    assert SUB_TILE % MOBA_BLOCK == 0
    assert SUB_TILE % (SGU_CHUNK * SGU_CHUNKS_PER_DOT) == 0
    assert A % (ATTN_HEADS_PER_STEP * HEAD_DIM) == 0
    assert (ATTN_HEADS_PER_STEP * HEAD_DIM) % LANES == 0
    assert A % LANES == 0 and W % LANES == 0 and LANES % (W // N_SGU_GROUPS) == 0
    assert w_sgu.shape[1:] == (N_SGU_GROUPS, SGU_CHUNK, SGU_CHUNK)
    assert w_in.shape[2] == 3 * A + 2 * W + 2 * d

    for l in range(depth):
        mod3, (win, wps) = _modulation(c, w_ada[l], b_ada[l], (w_in[l], w_proj_sgu[l]))
        b_full = jnp.repeat(b_sgu[l].T, W // N_SGU_GROUPS, axis=1)
        qT, k, vT, kmean, sa, gs = _inproj(
            x, mod3, g_mix[l], win, g_sgu[l], w_sgu[l], b_full, wps, attn_w=A, sgu_w=W)
        o, (wpa, wout, wff1, wff2) = _attention(
            qT, k, vT, kmean,
            (w_proj_attn[l], w_out[l], w_ff1[l], w_ff2[l]))
        x = _post(x, mod3, o, sa, gs, wpa, wout, g_ffn[l], wff1, wff2, g_final,
                  final_norm=(l == depth - 1))
    return x
```
